```python
import math
import jax, jax.numpy as jnp
from jax import lax
import numpy as np

D_MODEL = 1024
BATCH = 8
SEQ = 4096
DEPTH = 2

GRID_W = 64
CTX_LEN = 256
HEAD_DIM = 64
N_BRANCH = 4
BRANCH_W = D_MODEL // 4
POOL_WINDOWS = (2, 4, 8, 16)
POOL_GROUPS = 4
POOL_GW = BRANCH_W // POOL_GROUPS
DIFF_HEADS = 4
DIFF_DQ = HEAD_DIM // 2
DIFF_DV = HEAD_DIM
NAT_HEADS = 4
NAT_WIN_R = 8
NAT_WIN_C = 16
GQA_HEADS = 4
GQA_KV_HEADS = 2
ROPE_THETA = 10000.0
QBLK = 128
N_GROUPS = 4
EXPERTS_PER_GROUP = 8
N_EXPERTS = N_GROUPS * EXPERTS_PER_GROUP
TOP_K_IN_GROUP = 2
D_EXPERT = 512
RMS_EPS = 1e-6
NEG_INF = -1e30
SPLIT_WIDTHS = (BRANCH_W,
                DIFF_HEADS * HEAD_DIM, DIFF_HEADS * HEAD_DIM, DIFF_HEADS * DIFF_DV,
                NAT_HEADS * HEAD_DIM, NAT_HEADS * HEAD_DIM, NAT_HEADS * HEAD_DIM,
                GQA_HEADS * HEAD_DIM, GQA_KV_HEADS * HEAD_DIM, GQA_KV_HEADS * HEAD_DIM,
                N_BRANCH * D_MODEL)
IN_COLS = sum(SPLIT_WIDTHS)
CTX_KV_IDX = (2, 3, 5, 6, 8, 9)

kernel_name = 'hybrid_pool_diff_nat_gqa_hmoe_dit'


def _rmsnorm(x, g):
    xf = x.astype(jnp.float32)
    y = xf * lax.rsqrt(jnp.mean(xf * xf, axis=-1, keepdims=True) + RMS_EPS)
    return y.astype(x.dtype) * g


def _modulate(x, g, shift, scale):
    return _rmsnorm(x, g) * (1.0 + scale[:, None]) + shift[:, None]


def _split_cols(a):
    outs, o = [], 0
    for w in SPLIT_WIDTHS:
        outs.append(a[..., o:o + w])
        o += w
    return outs


def _rope_1d(x, pos):
    half = x.shape[-1] // 2
    inv = ROPE_THETA ** (-jnp.arange(half, dtype=jnp.float32) / half)
    ang = pos.astype(jnp.float32)[:, None] * inv[None, :]
    cos = jnp.cos(ang).astype(x.dtype)
    sin = jnp.sin(ang).astype(x.dtype)
    x1, x2 = x[..., :half], x[..., half:]
    return jnp.concatenate([x1 * cos - x2 * sin, x1 * sin + x2 * cos], axis=-1)


def _axial_rope(x):
    n, d = x.shape[-2], x.shape[-1]
    t = jnp.arange(n, dtype=jnp.int32)
    h = d // 2
    return jnp.concatenate([_rope_1d(x[..., :h], t // GRID_W), _rope_1d(x[..., h:], t % GRID_W)], axis=-1)


def _heads(a, n_heads):
    b, n, _ = a.shape
    return a.reshape(b, n, n_heads, -1).transpose(0, 2, 1, 3)


def _unheads(o):
    b, h, n, d = o.shape
    return o.transpose(0, 2, 1, 3).reshape(b, n, h * d)


def _sweep_queries(fn, q):
    n = q.shape[-2]
    nb = n // QBLK
    qb = jnp.moveaxis(q.reshape(q.shape[:-2] + (nb, QBLK, q.shape[-1])), -3, 0)
    out = jnp.moveaxis(lax.map(fn, qb), 0, -3)
    return out.reshape(out.shape[:-3] + (n, out.shape[-1]))


def _dense_core(q, k, v, scale):
    s = jnp.einsum('bhqd,bhkd->bhqk', q, k).astype(jnp.float32) * scale
    p = jax.nn.softmax(s, axis=-1)
    return jnp.einsum('bhqk,bhkd->bhqd', p.astype(v.dtype), v)


def _pool_mixer(u, pool_w, pool_scale):
    b, n, ch = u.shape
    cs = jnp.concatenate([jnp.zeros((b, 1, ch), jnp.float32), jnp.cumsum(u.astype(jnp.float32), axis=1)], axis=1)
    win = jnp.repeat(jnp.array(POOL_WINDOWS, jnp.int32), POOL_GW)
    t = jnp.arange(n, dtype=jnp.int32)[:, None]
    lo = jnp.clip(t - win // 2, 0, n)
    hi = jnp.clip(t + win - win // 2, 0, n)
    chan = jnp.arange(ch)[None, :]
    mean = (cs[:, hi, chan] - cs[:, lo, chan]) / (hi - lo).astype(jnp.float32)
    pooled = (mean - u.astype(jnp.float32)).astype(u.dtype)
    y = jnp.einsum('bngc,gcd->bngd', pooled.reshape(b, n, POOL_GROUPS, POOL_GW), pool_w)
    return y.reshape(b, n, ch) * pool_scale


def _diff_core(qb, k, v, lam):
    s = jnp.einsum('bhcqd,bhckd->bhcqk', qb, k).astype(jnp.float32) * (DIFF_DQ ** -0.5)
    p = jax.nn.softmax(s, axis=-1)
    pd = p[:, :, 0] - lam * p[:, :, 1]
    return jnp.einsum('bhqk,bhkd->bhqd', pd.astype(v.dtype), v)


def _diff_attention(q_l, k_l, v_l, q_c, k_c, v_c, lam_p, norm_g, layer, need_ctx):
    lam_init = 0.8 - 0.6 * math.exp(-0.3 * layer)
    lf = lam_p.astype(jnp.float32)
    lam = jnp.exp(jnp.sum(lf[0] * lf[1])) - jnp.exp(jnp.sum(lf[2] * lf[3])) + lam_init

    def qk_heads(a):
        b, n, _ = a.shape
        return a.reshape(b, n, DIFF_HEADS, 2, DIFF_DQ).transpose(0, 2, 3, 1, 4)

    def finish(o, dtype):
        return _unheads(_rmsnorm(o, norm_g) * (1.0 - lam_init)).astype(dtype)

    k_ctx = qk_heads(k_c)
    v_ctx = _heads(v_c, DIFF_HEADS)
    k_all = jnp.concatenate([_axial_rope(qk_heads(k_l)), k_ctx], axis=3)
    v_all = jnp.concatenate([_heads(v_l, DIFF_HEADS), v_ctx], axis=2)
    o_l = _sweep_queries(lambda qb: _diff_core(qb, k_all, v_all, lam), _axial_rope(qk_heads(q_l)))
    y_l = finish(o_l, q_l.dtype)
    y_c = finish(_diff_core(qk_heads(q_c), k_ctx, v_ctx, lam), q_c.dtype) if need_ctx else None
    return y_l, y_c


def _neighbourhood_attention(q_l, k_l, v_l, q_c, k_c, v_c, rpb, need_ctx):
    b, n, _ = q_l.shape
    rows = n // GRID_W
    kr = min(NAT_WIN_R, rows)
    kc = min(NAT_WIN_C, GRID_W)
    scale = HEAD_DIM ** -0.5

    def grid(a):
        return a.reshape(b, rows, GRID_W, NAT_HEADS, HEAD_DIM).transpose(0, 3, 1, 2, 4)

    qg, kg, vg = grid(q_l), grid(k_l), grid(v_l)
    k_ctx, v_ctx = _heads(k_c, NAT_HEADS), _heads(v_c, NAT_HEADS)
    r = jnp.arange(rows, dtype=jnp.int32)
    row_idx = jnp.clip(r - kr // 2, 0, rows - kr)[:, None] + jnp.arange(kr, dtype=jnp.int32)[None, :]
    dr = row_idx - r[:, None] + (NAT_WIN_R - 1)
    j = jnp.arange(GRID_W, dtype=jnp.int32)
    col_start = jnp.clip(j - kc // 2, 0, GRID_W - kc)
    valid = (j[None, :] >= col_start[:, None]) & (j[None, :] < col_start[:, None] + kc)
    dc = jnp.clip(j[None, :] - j[:, None], -(NAT_WIN_C - 1), NAT_WIN_C - 1) + (NAT_WIN_C - 1)

    def row_block(args):
        q_r, ridx, dr_r = args
        kb = kg[:, :, ridx]
        vb = vg[:, :, ridx]
        bias = rpb[:, dr_r[None, :, None], dc[:, None, :]].astype(jnp.float32)
        s_band = jnp.einsum('bhqd,bhnwd->bhqnw', q_r, kb).astype(jnp.float32) * scale + bias[None]
        s_band = jnp.where(valid[:, None, :], s_band, NEG_INF).reshape(b, NAT_HEADS, GRID_W, kr * GRID_W)
        s_ctx = jnp.einsum('bhqd,bhkd->bhqk', q_r, k_ctx).astype(jnp.float32) * scale
        p = jax.nn.softmax(jnp.concatenate([s_band, s_ctx], axis=-1), axis=-1)
        p_band = p[..., :kr * GRID_W].reshape(b, NAT_HEADS, GRID_W, kr, GRID_W).astype(vb.dtype)
        p_ctx = p[..., kr * GRID_W:].astype(v_ctx.dtype)
        return (jnp.einsum('bhqnw,bhnwd->bhqd', p_band, vb)
                + jnp.einsum('bhqk,bhkd->bhqd', p_ctx, v_ctx))

    o = lax.map(row_block, (jnp.moveaxis(qg, 2, 0), row_idx, dr))
    y_l = o.transpose(1, 0, 3, 2, 4).reshape(b, n, NAT_HEADS * HEAD_DIM)
    y_c = _unheads(_dense_core(_heads(q_c, NAT_HEADS), k_ctx, v_ctx, scale)) if need_ctx else None
    return y_l, y_c


def _gqa_core(qb, k, v):
    s = jnp.einsum('bhgqd,bhkd->bhgqk', qb, k).astype(jnp.float32) * (HEAD_DIM ** -0.5)
    p = jax.nn.softmax(s, axis=-1)
    return jnp.einsum('bhgqk,bhkd->bhgqd', p.astype(v.dtype), v)


def _gqa_attention(q_l, k_l, v_l, q_c, k_c, v_c, q_norm, k_norm, need_ctx):
    grp = GQA_HEADS // GQA_KV_HEADS

    def q_heads(a):
        b, n, _ = a.shape
        return _rmsnorm(a.reshape(b, n, GQA_KV_HEADS, grp, HEAD_DIM), q_norm).transpose(0, 2, 3, 1, 4)

    def k_heads(a):
        b, n, _ = a.shape
        return _rmsnorm(a.reshape(b, n, GQA_KV_HEADS, HEAD_DIM), k_norm).transpose(0, 2, 1, 3)

    def merge(o):
        b, h, g, n, d = o.shape
        return o.transpose(0, 3, 1, 2, 4).reshape(b, n, h * g * d)

    k_ctx, v_ctx = k_heads(k_c), _heads(v_c, GQA_KV_HEADS)
    k_all = jnp.concatenate([_axial_rope(k_heads(k_l)), k_ctx], axis=2)
    v_all = jnp.concatenate([_heads(v_l, GQA_KV_HEADS), v_ctx], axis=2)
    y_l = merge(_sweep_queries(lambda qb: _gqa_core(qb, k_all, v_all), _axial_rope(q_heads(q_l))))
    y_c = merge(_gqa_core(q_heads(q_c), k_ctx, v_ctx)) if need_ctx else None
    return y_l, y_c


def _merge_branches(branches, gate_logits, w_branch, w_out):
    b, n, _ = gate_logits.shape
    gates = jax.nn.sigmoid(gate_logits.astype(jnp.float32)).astype(gate_logits.dtype).reshape(b, n, N_BRANCH, D_MODEL)
    merged = gates[:, :, 0] * (branches[0] @ w_branch[0])
    for i in range(1, N_BRANCH):
        merged = merged + gates[:, :, i] * (branches[i] @ w_branch[i])
    return merged @ w_out


def _mixer(h_lat, h_ctx, w_in, pool_w, pool_scale, diff_lam, diff_g, rpb, q_norm, k_norm, w_branch, w_out, layer, need_ctx):
    pl = _split_cols(h_lat @ w_in)
    if need_ctx:
        pc = _split_cols(h_ctx @ w_in)
    else:
        pc = [h_ctx @ blk if i in CTX_KV_IDX else None for i, blk in enumerate(_split_cols(w_in))]
    d_l, d_c = _diff_attention(pl[1], pl[2], pl[3], pc[1], pc[2], pc[3], diff_lam, diff_g, layer, need_ctx)
    n_l, n_c = _neighbourhood_attention(pl[4], pl[5], pl[6], pc[4], pc[5], pc[6], rpb, need_ctx)
    g_l, g_c = _gqa_attention(pl[7], pl[8], pl[9], pc[7], pc[8], pc[9], q_norm, k_norm, need_ctx)
    out_l = _merge_branches((_pool_mixer(pl[0], pool_w, pool_scale), d_l, n_l, g_l), pl[10], w_branch, w_out)
    out_c = None
    if need_ctx:
        out_c = _merge_branches((_pool_mixer(pc[0], pool_w, pool_scale), d_c, n_c, g_c), pc[10], w_branch, w_out)
    return out_l, out_c


def _moe(h, w_rg, b_rg, w_re, b_re, w_gate, w_up, w_down):
    b, n, d = h.shape
    t = h.reshape(b * n, d)
    nt = t.shape[0]
    tok = jnp.arange(nt)
    g_logits = (t @ w_rg).astype(jnp.float32) + b_rg.astype(jnp.float32)
    g_prob = jax.nn.softmax(g_logits, axis=-1)
    grp = jnp.argmax(g_logits, axis=-1)
    w_grp = g_prob[tok, grp][:, None]
    e_logits = ((t @ w_re).astype(jnp.float32) + b_re.astype(jnp.float32)).reshape(nt, N_GROUPS, EXPERTS_PER_GROUP)
    top_v, top_i = lax.top_k(e_logits[tok, grp], TOP_K_IN_GROUP)
    weights = jax.nn.softmax(top_v, axis=-1) * w_grp
    eid = grp[:, None] * EXPERTS_PER_GROUP + top_i
    comb = jnp.sum(jax.nn.one_hot(eid, N_EXPERTS, dtype=jnp.float32) * weights[..., None], axis=1).astype(h.dtype)
    y = jnp.zeros_like(t)
    for e in range(N_EXPERTS):
        a = jax.nn.silu(t @ w_gate[e]) * (t @ w_up[e])
        y = y + comb[:, e:e + 1] * (a @ w_down[e])
    return y.reshape(b, n, d)


def setup_inputs(seed: int = 0) -> dict:
    key = jax.random.key(seed)
    ks = iter(jax.random.split(key, 32))
    L, D = DEPTH, D_MODEL

    def nrm(shape, s):
        return jax.random.normal(next(ks), shape, jnp.float32) * s

    return {
        'x': nrm((BATCH, SEQ, D), 1.0),
        'c': nrm((BATCH, D), 1.0),
        'ctx': nrm((BATCH, CTX_LEN, D), 1.0),
        'c_ctx': nrm((D,), 1.0),
        'w_mod': nrm((L, D, 6 * D), 0.5 * D ** -0.5),
        'b_mod': nrm((L, 6 * D), 0.02),
        'g_mix': 1.0 + nrm((L, D), 0.02),
        'g_ffn': 1.0 + nrm((L, D), 0.02),
        'w_in': nrm((L, D, IN_COLS), D ** -0.5),
        'pool_w': nrm((L, POOL_GROUPS, POOL_GW, POOL_GW), POOL_GW ** -0.5),
        'pool_scale': 1.0 + nrm((L, BRANCH_W), 0.02),
        'diff_lambda': nrm((L, 4, DIFF_DQ), 0.1),
        'diff_norm_g': 1.0 + nrm((L, DIFF_DV), 0.02),
        'nat_rpb': nrm((L, NAT_HEADS, 2 * NAT_WIN_R - 1, 2 * NAT_WIN_C - 1), 0.02),
        'gqa_q_norm': 1.0 + nrm((L, HEAD_DIM), 0.02),
        'gqa_k_norm': 1.0 + nrm((L, HEAD_DIM), 0.02),
        'w_branch': nrm((L, N_BRANCH, BRANCH_W, D), BRANCH_W ** -0.5),
        'w_out': nrm((L, D, D), D ** -0.5),
        'w_router_group': nrm((L, D, N_GROUPS), D ** -0.5),
        'b_router_group': nrm((L, N_GROUPS), 0.01),
        'w_router_expert': nrm((L, D, N_EXPERTS), D ** -0.5),
        'b_router_expert': nrm((L, N_EXPERTS), 0.01),
        'w_exp_gate': nrm((L, N_EXPERTS, D, D_EXPERT), D ** -0.5),
        'w_exp_up': nrm((L, N_EXPERTS, D, D_EXPERT), D ** -0.5),
        'w_exp_down': nrm((L, N_EXPERTS, D_EXPERT, D), D_EXPERT ** -0.5),
        'g_final': 1.0 + nrm((D,), 0.02),
    }


def reference(x, c, ctx, c_ctx, w_mod, b_mod, g_mix, g_ffn, w_in, pool_w, pool_scale, diff_lambda, diff_norm_g,
              nat_rpb, gqa_q_norm, gqa_k_norm, w_branch, w_out, w_router_group, b_router_group, w_router_expert,
              b_router_expert, w_exp_gate, w_exp_up, w_exp_down, g_final):
    x_lat, x_ctx = x, ctx
    s_lat = jax.nn.silu(c)
    s_ctx = jax.nn.silu(c_ctx)[None]
    for l in range(DEPTH):
        need_ctx = l < DEPTH - 1
        sh1, sc1, gt1, sh2, sc2, gt2 = jnp.split(s_lat @ w_mod[l] + b_mod[l], 6, axis=-1)
        csh1, csc1, cgt1, csh2, csc2, cgt2 = jnp.split(s_ctx @ w_mod[l] + b_mod[l], 6, axis=-1)
        h_lat = _modulate(x_lat, g_mix[l], sh1, sc1)
        h_ctx = _modulate(x_ctx, g_mix[l], csh1, csc1)
        o_lat, o_ctx = _mixer(h_lat, h_ctx, w_in[l], pool_w[l], pool_scale[l], diff_lambda[l], diff_norm_g[l],
                              nat_rpb[l], gqa_q_norm[l], gqa_k_norm[l], w_branch[l], w_out[l], l, need_ctx)
        x_lat = x_lat + gt1[:, None] * o_lat
        x_lat = x_lat + gt2[:, None] * _moe(_modulate(x_lat, g_ffn[l], sh2, sc2), w_router_group[l], b_router_group[l],
                                            w_router_expert[l], b_router_expert[l], w_exp_gate[l], w_exp_up[l], w_exp_down[l])
        if need_ctx:
            x_ctx = x_ctx + cgt1[:, None] * o_ctx
            x_ctx = x_ctx + cgt2[:, None] * _moe(_modulate(x_ctx, g_ffn[l], csh2, csc2), w_router_group[l], b_router_group[l],
                                                 w_router_expert[l], b_router_expert[l], w_exp_gate[l], w_exp_up[l], w_exp_down[l])
    return _rmsnorm(x_lat, g_final)
```

```python
import functools
import math

import numpy as np
import jax
import jax.numpy as jnp
from jax import lax
from jax.experimental import pallas as pl
from jax.experimental.pallas import tpu as pltpu

GRID_W = 64
HEAD_DIM = 64
BRANCH_W = 256
N_BRANCH = 4
POOL_HALF_WINDOWS = (1, 2, 4, 8)
POOL_GW = 64
DIFF_DQ = 32
NAT_WIN_R = 8
NAT_WIN_C = 16
N_GROUPS = 4
EXPERTS_PER_GROUP = 8
N_EXPERTS = 32
D_EXPERT = 512
ROPE_THETA = 10000.0
RMS_EPS = 1e-6
NEG_INF = -1e30
LOG2E = 1.4426950408889634

LANES = 128
TOKEN_TILE = 256
EXPERT_TILE = 256
VMEM_LIMIT_BYTES = 56 * 1024 * 1024

PROJ_COLS = 2304
COL_DQ, COL_DK, COL_DV = 0, 256, 512
COL_NQ, COL_NK, COL_NV = 768, 1024, 1280
COL_GQ, COL_GK, COL_GV = 1536, 2048, 2176

F32 = jnp.float32
BF16 = jnp.bfloat16
HIGHEST = lax.Precision.HIGHEST


def _params(*sem):
    return pltpu.CompilerParams(dimension_semantics=sem, vmem_limit_bytes=VMEM_LIMIT_BYTES)


def _lane_iota(shape):
    return lax.broadcasted_iota(jnp.int32, shape, len(shape) - 1)


def _dot_nt(a, b):
    return lax.dot_general(a, b, (((1,), (1,)), ((), ())), preferred_element_type=F32)


def _modulated_norm(x, gain, shift, scale):
    ms = jnp.mean(x * x, axis=-1, keepdims=True)
    return (x * lax.rsqrt(ms + RMS_EPS)) * gain * (1.0 + scale) + shift


def _mod_kernel(c_ref, w_ref, b_ref, o_ref):
    c = c_ref[...]
    s = c * jax.nn.sigmoid(c)
    o_ref[0] = jnp.dot(s, w_ref[0], preferred_element_type=F32, precision=HIGHEST) + b_ref[0]


def _modulation(cc, w_mod, b_mod):
    depth, d, n6 = w_mod.shape
    rows = cc.shape[0]
    tn = 512
    return pl.pallas_call(
        _mod_kernel,
        grid=(depth, n6 // tn),
        in_specs=[pl.BlockSpec((rows, d), lambda l, j: (0, 0)),
                  pl.BlockSpec((1, d, tn), lambda l, j: (l, 0, j)),
                  pl.BlockSpec((1, 1, tn), lambda l, j: (l, 0, j))],
        out_specs=pl.BlockSpec((1, rows, tn), lambda l, j: (l, 0, j)),
        out_shape=jax.ShapeDtypeStruct((depth, rows, n6), F32),
        compiler_params=_params("arbitrary", "arbitrary"),
        name="modulation",
    )(cc, w_mod, b_mod.reshape(depth, 1, n6))


def _rope_tables(n_lat, n_ctx, width, vec_dim):
    t = np.arange(n_lat)
    row, col = t // GRID_W, t % GRID_W
    j = np.arange(width) % vec_dim
    half = vec_dim // 2
    quarter = half // 2
    jj = j % half
    inv = ROPE_THETA ** (-(jj % quarter).astype(np.float64) / quarter)
    pos = np.where((j < half)[None, :], row[:, None], col[:, None]).astype(np.float64)
    ang = pos * inv[None, :]
    sign = np.where(jj < quarter, -1.0, 1.0)[None, :]
    cos = np.concatenate([np.cos(ang), np.ones((n_ctx, width))], axis=0)
    sin = np.concatenate([np.sin(ang) * sign, np.zeros((n_ctx, width))], axis=0)
    return jnp.asarray(cos, F32), jnp.asarray(sin, F32)


def _rope(x, cos, sin, quarter):
    lane = _lane_iota(x.shape)
    first = (lane % (2 * quarter)) < quarter
    up = pltpu.roll(x, LANES - quarter, 1)
    down = pltpu.roll(x, quarter, 1)
    return x * cos + jnp.where(first, up, down) * sin


def _head_rms(x, ones_bd, gain):
    ms = jnp.dot(x * x, ones_bd, preferred_element_type=F32, precision=HIGHEST)
    return x * lax.rsqrt(ms + RMS_EPS) * gain


def _inproj_kernel(x_ref, mod_ref, g_ref, w_ref, cd_ref, sd_ref, cg_ref, sg_ref, ones_ref, qn_ref, kn_ref,
                   u_ref, p_ref):
    x = x_ref[0]
    h = _modulated_norm(x, g_ref[...], mod_ref[0, 0, 0:1, :], mod_ref[0, 0, 1:2, :]).astype(BF16)

    def proj(col, width=LANES):
        return jnp.dot(h, w_ref[:, col:col + width], preferred_element_type=F32)

    for b in range(2):
        u_ref[0, :, b * LANES:(b + 1) * LANES] = proj(b * LANES)
    for dst, src in ((COL_DQ, 256), (COL_DK, 512)):
        for b in range(2):
            sl = slice(b * LANES, (b + 1) * LANES)
            y = _rope(proj(src + b * LANES), cd_ref[:, sl], sd_ref[:, sl], DIFF_DQ // 4)
            p_ref[0, :, dst + b * LANES:dst + (b + 1) * LANES] = y.astype(BF16)
    for dst, src, width in ((COL_DV, 768, 256), (COL_NQ, 1024, 256), (COL_NK, 1280, 256), (COL_NV, 1536, 256),
                            (COL_GV, 2176, 128)):
        for b in range(width // LANES):
            p_ref[0, :, dst + b * LANES:dst + (b + 1) * LANES] = proj(src + b * LANES).astype(BF16)
    ones_bd = ones_ref[...]
    lane = _lane_iota((x.shape[0], LANES))
    low = lane < HEAD_DIM
    for b in range(2):
        sl = slice(b * LANES, (b + 1) * LANES)
        y = _head_rms(proj(1792 + b * LANES), ones_bd, qn_ref[...])
        y = _rope(y, cg_ref[:, sl], sg_ref[:, sl], HEAD_DIM // 4)
        swapped = pltpu.roll(y, HEAD_DIM, 1)
        keep = low if b == 0 else jnp.logical_not(low)
        even = jnp.where(keep, y if b == 0 else swapped, 0.0)
        odd = jnp.where(keep, swapped if b == 0 else y, 0.0)
        p_ref[0, :, COL_GQ + (2 * b) * LANES:COL_GQ + (2 * b + 1) * LANES] = even.astype(BF16)
        p_ref[0, :, COL_GQ + (2 * b + 1) * LANES:COL_GQ + (2 * b + 2) * LANES] = odd.astype(BF16)
    y = _head_rms(proj(2048), ones_bd, kn_ref[...])
    y = _rope(y, cg_ref[:, 0:LANES], sg_ref[:, 0:LANES], HEAD_DIM // 4)
    p_ref[0, :, COL_GK:COL_GK + LANES] = y.astype(BF16)


def _inproj(xc, modv, g_mix, w_proj, tabs, ones_bd, q_norm, k_norm, n_lat):
    bsz, n_all, d = xc.shape
    tm = TOKEN_TILE
    lat_tiles = n_lat // tm
    cd, sd, cg, sg = tabs
    full = lambda shape: pl.BlockSpec(shape, lambda b, t: (0,) * len(shape))
    tab = pl.BlockSpec((tm, 256), lambda b, t: (t, 0))
    return pl.pallas_call(
        _inproj_kernel,
        grid=(bsz, n_all // tm),
        in_specs=[pl.BlockSpec((1, tm, d), lambda b, t: (b, t, 0)),
                  pl.BlockSpec((1, 1, 6, d), lambda b, t: (b, jnp.minimum(t // lat_tiles, 1), 0, 0)),
                  full((1, d)), full((d, PROJ_COLS)), tab, tab, tab, tab,
                  full((LANES, LANES)), full((1, LANES)), full((1, LANES))],
        out_specs=[pl.BlockSpec((1, tm, BRANCH_W), lambda b, t: (b, t, 0)),
                   pl.BlockSpec((1, tm, PROJ_COLS), lambda b, t: (b, t, 0))],
        out_shape=[jax.ShapeDtypeStruct((bsz, n_all, BRANCH_W), F32),
                   jax.ShapeDtypeStruct((bsz, n_all, PROJ_COLS), BF16)],
        compiler_params=_params("parallel", "parallel"),
        name="inproj",
    )(xc, modv, g_mix, w_proj, cd, sd, cg, sg, ones_bd, q_norm, k_norm)


def _pool_kernel(prev_ref, u_ref, next_ref, w_ref, scale_ref, o_ref, pad_ref, *, n_lat, n_all):
    t = pl.program_id(1)
    tm = TOKEN_TILE
    lat_tiles = n_lat // tm
    all_tiles = n_all // tm
    halo = 8
    u = u_ref[0]
    first = jnp.logical_or(t == 0, t == lat_tiles)
    last = jnp.logical_or(t == lat_tiles - 1, t == all_tiles - 1)
    pad_ref[0:halo, :] = jnp.where(first, 0.0, prev_ref[0])
    pad_ref[halo:halo + tm, :] = u
    pad_ref[halo + tm:2 * halo + tm, :] = jnp.where(last, 0.0, next_ref[0])

    def shifted(d):
        return pad_ref[halo + d:halo + d + tm, :]

    sums = []
    acc = None
    for hw in POOL_HALF_WINDOWS:
        lo = hw // 2 if acc is not None else 0
        for d in range(lo, hw):
            term = shifted(d) + shifted(-d - 1)
            acc = term if acc is None else acc + term
        sums.append(acc)
    lane = _lane_iota((tm, BRANCH_W))
    group = lane // POOL_GW
    win = jnp.where(group == 0, sums[0], jnp.where(group == 1, sums[1], jnp.where(group == 2, sums[2], sums[3])))
    half = jnp.where(group == 0, 1, jnp.where(group == 1, 2, jnp.where(group == 2, 4, 8)))
    seg_start = jnp.where(t >= lat_tiles, n_lat, 0)
    seg_len = jnp.where(t >= lat_tiles, n_all - n_lat, n_lat)
    pos = t * tm - seg_start + lax.broadcasted_iota(jnp.int32, (tm, BRANCH_W), 0)
    cnt = jnp.minimum(pos + half, seg_len) - jnp.maximum(pos - half, 0)
    pooled = (win / cnt.astype(F32) - u).astype(BF16)
    y = jnp.dot(pooled, w_ref[...], preferred_element_type=F32) * scale_ref[...]
    o_ref[0] = y.astype(BF16)


def _pool(u, w_bd, pool_scale, n_lat):
    bsz, n_all, ch = u.shape
    tm = TOKEN_TILE
    halo = 8
    per = tm // halo
    nblk = n_all // halo
    return pl.pallas_call(
        functools.partial(_pool_kernel, n_lat=n_lat, n_all=n_all),
        grid=(bsz, n_all // tm),
        in_specs=[pl.BlockSpec((1, halo, ch), lambda b, t: (b, jnp.maximum(t * per - 1, 0), 0)),
                  pl.BlockSpec((1, tm, ch), lambda b, t: (b, t, 0)),
                  pl.BlockSpec((1, halo, ch), lambda b, t: (b, jnp.minimum((t + 1) * per, nblk - 1), 0)),
                  pl.BlockSpec((ch, ch), lambda b, t: (0, 0)),
                  pl.BlockSpec((1, ch), lambda b, t: (0, 0))],
        out_specs=pl.BlockSpec((1, tm, ch), lambda b, t: (b, t, 0)),
        out_shape=jax.ShapeDtypeStruct((bsz, n_all, ch), BF16),
        scratch_shapes=[pltpu.VMEM((tm + 2 * halo, ch), F32)],
        compiler_params=_params("parallel", "parallel"),
        name="pool",
    )(u, u, u, w_bd, pool_scale)


def _softmax_pass(qm, k_ref, s_scr, p_scr, slot, lo, n_chunks, tk, scale_log2):
    rows = qm.shape[0]

    def scores(j, mpart):
        kj = k_ref[0, pl.ds(pl.multiple_of(j * tk, tk), tk), :]
        s = _dot_nt(qm, kj) * scale_log2
        s_scr[j] = s
        for b in range(tk // LANES):
            mpart = jnp.maximum(mpart, s[:, b * LANES:(b + 1) * LANES])
        return mpart

    mpart = lax.fori_loop(lo, n_chunks, scores, jnp.full((rows, LANES), -jnp.inf, F32))
    m = jnp.max(mpart, axis=-1, keepdims=True)

    def probs(j, lpart):
        p = jnp.exp2(s_scr[j] - m)
        p_scr[slot, j] = p.astype(BF16)
        for b in range(tk // LANES):
            lpart = lpart + p[:, b * LANES:(b + 1) * LANES]
        return lpart

    lpart = lax.fori_loop(lo, n_chunks, probs, jnp.zeros((rows, LANES), F32))
    return jnp.sum(lpart, axis=-1, keepdims=True)


def _diff_kernel(lam_ref, g_ref, ones_ref, q_ref, k_ref, v_ref, o_ref, s_scr, p_scr, *, n_lat, n_all, tq, tk,
                 lam_init):
    i = pl.program_id(2)
    hh = pl.program_id(3)
    n_chunks = n_all // tk
    lo = jnp.where(i * tq >= n_lat, n_lat // tk, 0)
    lf = lam_ref[...]
    lam = (jnp.exp(jnp.sum(lf[0:1] * lf[1:2], axis=-1, keepdims=True))
           - jnp.exp(jnp.sum(lf[2:3] * lf[3:4], axis=-1, keepdims=True)) + lam_init)
    q = q_ref[0]
    lane = _lane_iota((tq, LANES))
    scale_log2 = (DIFF_DQ ** -0.5) * LOG2E
    sums = []
    for c in range(2):
        start = hh * HEAD_DIM + c * DIFF_DQ
        qm = jnp.where(jnp.logical_and(lane >= start, lane < start + DIFF_DQ), q, jnp.zeros_like(q))
        sums.append(_softmax_pass(qm, k_ref, s_scr, p_scr, c, lo, n_chunks, tk, scale_log2))
    ratio = lam * sums[0] / sums[1]

    def weighted(j, acc):
        pd = p_scr[0, j].astype(F32) - ratio * p_scr[1, j].astype(F32)
        vj = v_ref[0, pl.ds(pl.multiple_of(j * tk, tk), tk), :]
        return acc + jnp.dot(pd.astype(BF16), vj, preferred_element_type=F32)

    o = lax.fori_loop(lo, n_chunks, weighted, jnp.zeros((tq, LANES), F32)) / sums[0]
    ms = jnp.dot(o * o, ones_ref[...], preferred_element_type=F32, precision=HIGHEST)
    y = (o * lax.rsqrt(ms + RMS_EPS) * g_ref[...] * (1.0 - lam_init)).astype(BF16)
    mine = jnp.logical_and(lane >= hh * HEAD_DIM, lane < (hh + 1) * HEAD_DIM)

    @pl.when(hh == 0)
    def _():
        o_ref[0] = jnp.where(mine, y, jnp.zeros_like(y))

    @pl.when(hh == 1)
    def _():
        o_ref[0] = jnp.where(mine, y, o_ref[0])


def _diff_attention(proj, lam_p, norm_g, ones_bd, n_lat, layer):
    bsz, n_all, _ = proj.shape
    tq = tk = TOKEN_TILE
    lam_init = 0.8 - 0.6 * math.exp(-0.3 * layer)
    qb, kb, vb = COL_DQ // LANES, COL_DK // LANES, COL_DV // LANES
    return pl.pallas_call(
        functools.partial(_diff_kernel, n_lat=n_lat, n_all=n_all, tq=tq, tk=tk, lam_init=lam_init),
        grid=(bsz, 2, n_all // tq, 2),
        in_specs=[pl.BlockSpec(lam_p.shape, lambda b, p, i, h: (0, 0)),
                  pl.BlockSpec((1, LANES), lambda b, p, i, h: (0, 0)),
                  pl.BlockSpec((LANES, LANES), lambda b, p, i, h: (0, 0)),
                  pl.BlockSpec((1, tq, LANES), lambda b, p, i, h: (b, i, qb + p)),
                  pl.BlockSpec((1, n_all, LANES), lambda b, p, i, h: (b, 0, kb + p)),
                  pl.BlockSpec((1, n_all, LANES), lambda b, p, i, h: (b, 0, vb + p))],
        out_specs=pl.BlockSpec((1, tq, LANES), lambda b, p, i, h: (b, i, p)),
        out_shape=jax.ShapeDtypeStruct((bsz, n_all, BRANCH_W), BF16),
        scratch_shapes=[pltpu.VMEM((n_all // tk, tq, tk), F32), pltpu.VMEM((2, n_all // tk, tq, tk), BF16)],
        compiler_params=_params("parallel", "parallel", "parallel", "arbitrary"),
        name="diff_attention",
    )(lam_p, norm_g, ones_bd, proj, proj, proj)


def _gqa_kernel(q_ref, k_ref, v_ref, o_ref, s_scr, p_scr, *, n_lat, n_all, tq, tk):
    kv = pl.program_id(1)
    i = pl.program_id(2)
    g = pl.program_id(3)
    n_chunks = n_all // tk
    lo = jnp.where(i * tq >= n_lat, n_lat // tk, 0)
    scale_log2 = (HEAD_DIM ** -0.5) * LOG2E
    total = _softmax_pass(q_ref[0], k_ref, s_scr, p_scr, 0, lo, n_chunks, tk, scale_log2)

    def weighted(j, acc):
        vj = v_ref[0, pl.ds(pl.multiple_of(j * tk, tk), tk), :]
        return acc + jnp.dot(p_scr[0, j], vj, preferred_element_type=F32)

    o = lax.fori_loop(lo, n_chunks, weighted, jnp.zeros((tq, LANES), F32)) / total
    o = jnp.where(kv == g, o, pltpu.roll(o, HEAD_DIM, 1)).astype(BF16)
    lane = _lane_iota((tq, LANES))
    mine = jnp.logical_and(lane >= g * HEAD_DIM, lane < (g + 1) * HEAD_DIM)

    @pl.when(g == 0)
    def _():
        o_ref[0] = jnp.where(mine, o, jnp.zeros_like(o))

    @pl.when(g == 1)
    def _():
        o_ref[0] = jnp.where(mine, o, o_ref[0])


def _gqa_attention(proj, n_lat):
    bsz, n_all, _ = proj.shape
    tq = tk = TOKEN_TILE
    qb, kb, vb = COL_GQ // LANES, COL_GK // LANES, COL_GV // LANES
    return pl.pallas_call(
        functools.partial(_gqa_kernel, n_lat=n_lat, n_all=n_all, tq=tq, tk=tk),
        grid=(bsz, 2, n_all // tq, 2),
        in_specs=[pl.BlockSpec((1, tq, LANES), lambda b, p, i, h: (b, i, qb + 2 * p + h)),
                  pl.BlockSpec((1, n_all, LANES), lambda b, p, i, h: (b, 0, kb)),
                  pl.BlockSpec((1, n_all, LANES), lambda b, p, i, h: (b, 0, vb))],
        out_specs=pl.BlockSpec((1, tq, LANES), lambda b, p, i, h: (b, i, p)),
        out_shape=jax.ShapeDtypeStruct((bsz, n_all, BRANCH_W), BF16),
        scratch_shapes=[pltpu.VMEM((n_all // tk, tq, tk), F32), pltpu.VMEM((1, n_all // tk, tq, tk), BF16)],
        compiler_params=_params("parallel", "parallel", "parallel", "arbitrary"),
        name="gqa_attention",
    )(proj, proj, proj)


def _nat_bias(rpb):
    kc = NAT_WIN_C
    j = np.arange(GRID_W)
    col_start = np.clip(j - kc // 2, 0, GRID_W - kc)
    valid = (j[None, :] >= col_start[:, None]) & (j[None, :] < col_start[:, None] + kc)
    dc = np.clip(j[None, :] - j[:, None], -(kc - 1), kc - 1) + (kc - 1)
    tab = rpb[:, :, dc]
    tab = jnp.where(jnp.asarray(valid)[None, None], tab.astype(F32), NEG_INF)
    heads = rpb.shape[0]
    slabs = []
    for dr0 in range(NAT_WIN_R):
        blk = tab[:, dr0:dr0 + NAT_WIN_R]
        slabs.append(blk.transpose(0, 2, 1, 3).reshape(heads * GRID_W, NAT_WIN_R * GRID_W))
    return jnp.stack(slabs)


def _nat_kernel(bias_ref, q_ref, k_ref, v_ref, o_ref, *, n_lat, n_all):
    w = GRID_W
    rows = n_lat // w
    heads = BRANCH_W // HEAD_DIM
    scale = HEAD_DIM ** -0.5
    lane = _lane_iota((w, BRANCH_W))
    head_of_lane = lane // HEAD_DIM
    k_ctx = k_ref[0, n_lat:n_all, :]
    v_ctx = v_ref[0, n_lat:n_all, :]

    def stacked_queries(start):
        q = q_ref[0, pl.ds(start, w), :]
        return jnp.concatenate([jnp.where(head_of_lane == h, q, jnp.zeros_like(q)) for h in range(heads)], axis=0)

    def unstack(o):
        out = jnp.zeros((w, BRANCH_W), F32)
        for h in range(heads):
            out = jnp.where(head_of_lane == h, o[h * w:(h + 1) * w, :], out)
        return out

    def grid_row(r, carry):
        r0 = jnp.clip(r - NAT_WIN_R // 2, 0, rows - NAT_WIN_R)
        dr0 = r0 - r + (NAT_WIN_R - 1)
        qs = stacked_queries(pl.multiple_of(r * w, w))
        kb = k_ref[0, pl.ds(pl.multiple_of(r0 * w, w), NAT_WIN_R * w), :]
        vb = v_ref[0, pl.ds(pl.multiple_of(r0 * w, w), NAT_WIN_R * w), :]
        s_band = _dot_nt(qs, kb) * scale + bias_ref[dr0]
        s_ctx = _dot_nt(qs, k_ctx) * scale
        m = jnp.maximum(jnp.max(s_band, axis=-1, keepdims=True), jnp.max(s_ctx, axis=-1, keepdims=True))
        p_band = jnp.exp(s_band - m)
        p_ctx = jnp.exp(s_ctx - m)
        total = jnp.sum(p_band, axis=-1, keepdims=True) + jnp.sum(p_ctx, axis=-1, keepdims=True)
        o = (jnp.dot(p_band.astype(BF16), vb, preferred_element_type=F32)
             + jnp.dot(p_ctx.astype(BF16), v_ctx, preferred_element_type=F32)) / total
        o_ref[0, pl.ds(pl.multiple_of(r * w, w), w), :] = unstack(o).astype(BF16)
        return carry

    lax.fori_loop(0, rows, grid_row, 0)

    def ctx_block(cb, carry):
        start = pl.multiple_of(n_lat + cb * w, w)
        qs = stacked_queries(start)
        s = _dot_nt(qs, k_ctx) * scale
        p = jnp.exp(s - jnp.max(s, axis=-1, keepdims=True))
        o = jnp.dot(p.astype(BF16), v_ctx, preferred_element_type=F32) / jnp.sum(p, axis=-1, keepdims=True)
        o_ref[0, pl.ds(start, w), :] = unstack(o).astype(BF16)
        return carry

    lax.fori_loop(0, (n_all - n_lat) // w, ctx_block, 0)


def _nat_attention(proj, bias, n_lat):
    bsz, n_all, _ = proj.shape
    qb, kb, vb = COL_NQ // BRANCH_W, COL_NK // BRANCH_W, COL_NV // BRANCH_W
    return pl.pallas_call(
        functools.partial(_nat_kernel, n_lat=n_lat, n_all=n_all),
        grid=(bsz,),
        in_specs=[pl.BlockSpec(bias.shape, lambda b: (0, 0, 0)),
                  pl.BlockSpec((1, n_all, BRANCH_W), lambda b: (b, 0, qb)),
                  pl.BlockSpec((1, n_all, BRANCH_W), lambda b: (b, 0, kb)),
                  pl.BlockSpec((1, n_all, BRANCH_W), lambda b: (b, 0, vb))],
        out_specs=pl.BlockSpec((1, n_all, BRANCH_W), lambda b: (b, 0, 0)),
        out_shape=jax.ShapeDtypeStruct((bsz, n_all, BRANCH_W), BF16),
        compiler_params=_params("parallel"),
        name="nat_attention",
    )(bias, proj, proj, proj)


def _route(logits):
    lane = _lane_iota(logits.shape)
    lane_f = lane.astype(F32)
    far = float(LANES)
    gl = jnp.where(lane < N_GROUPS, logits, -jnp.inf)
    gmax = jnp.max(gl, axis=-1, keepdims=True)
    grp = jnp.min(jnp.where(gl == gmax, lane_f, far), axis=-1, keepdims=True)
    w_grp = 1.0 / jnp.sum(jnp.exp(gl - gmax), axis=-1, keepdims=True)
    first = N_GROUPS + EXPERTS_PER_GROUP * grp
    el = jnp.where(jnp.logical_and(lane_f >= first, lane_f < first + EXPERTS_PER_GROUP), logits, -jnp.inf)
    v1 = jnp.max(el, axis=-1, keepdims=True)
    i1 = jnp.min(jnp.where(el == v1, lane_f, far), axis=-1, keepdims=True)
    el2 = jnp.where(lane_f == i1, -jnp.inf, el)
    v2 = jnp.max(el2, axis=-1, keepdims=True)
    i2 = jnp.min(jnp.where(el2 == v2, lane_f, far), axis=-1, keepdims=True)
    t = jnp.exp(v2 - v1)
    w1 = w_grp / (1.0 + t)
    w2 = w_grp * t / (1.0 + t)
    ids = jnp.where(lane == 0, i1 - N_GROUPS, jnp.where(lane == 1, i2 - N_GROUPS, 0.0)).astype(jnp.int32)
    wts = jnp.where(lane == 0, w1, jnp.where(lane == 1, w2, 0.0))
    return ids, wts


def _merge_kernel(x_ref, mod_ref, gm_ref, gf_ref, b0_ref, b1_ref, b2_ref, b3_ref, wg_ref, wb_ref, wo_ref,
                  wr_ref, br_ref, x1_ref, h2_ref, ids_ref, wts_ref):
    d = x_ref.shape[-1]
    x = x_ref[0]
    mod = lambda k: mod_ref[0, 0, k:k + 1, :]
    h = _modulated_norm(x, gm_ref[...], mod(0), mod(1)).astype(BF16)
    merged = None
    for i, b_ref in enumerate((b0_ref, b1_ref, b2_ref, b3_ref)):
        gate = jax.nn.sigmoid(jnp.dot(h, wg_ref[:, i * d:(i + 1) * d], preferred_element_type=F32))
        term = gate * jnp.dot(b_ref[0], wb_ref[i], preferred_element_type=F32)
        merged = term if merged is None else merged + term
    o = jnp.dot(merged.astype(BF16), wo_ref[...], preferred_element_type=F32)
    x1 = x + mod(2) * o
    x1_ref[0] = x1
    h2 = _modulated_norm(x1, gf_ref[...], mod(3), mod(4))
    h2_ref[0] = h2
    logits = jnp.dot(h2, wr_ref[...], preferred_element_type=F32, precision=HIGHEST) + br_ref[...]
    ids, wts = _route(logits)
    ids_ref[0] = ids
    wts_ref[0] = wts


def _merge(xc, modv, g_mix, g_ffn, branches, w_gate, w_branch, w_out, w_router, b_router, n_lat):
    bsz, n_all, d = xc.shape
    tm = TOKEN_TILE
    lat_tiles = n_lat // tm
    tile = lambda width: pl.BlockSpec((1, tm, width), lambda b, t: (b, t, 0))
    full = lambda shape: pl.BlockSpec(shape, lambda b, t: (0,) * len(shape))
    return pl.pallas_call(
        _merge_kernel,
        grid=(bsz, n_all // tm),
        in_specs=[tile(d), pl.BlockSpec((1, 1, 6, d), lambda b, t: (b, jnp.minimum(t // lat_tiles, 1), 0, 0)),
                  full((1, d)), full((1, d)),
                  tile(BRANCH_W), tile(BRANCH_W), tile(BRANCH_W), tile(BRANCH_W),
                  full(w_gate.shape), full(w_branch.shape), full(w_out.shape),
                  full(w_router.shape), full(b_router.shape)],
        out_specs=[tile(d), tile(d), tile(LANES), tile(LANES)],
        out_shape=[jax.ShapeDtypeStruct((bsz, n_all, d), F32), jax.ShapeDtypeStruct((bsz, n_all, d), F32),
                   jax.ShapeDtypeStruct((bsz, n_all, LANES), jnp.int32),
                   jax.ShapeDtypeStruct((bsz, n_all, LANES), F32)],
        compiler_params=_params("parallel", "parallel"),
        name="merge_router",
    )(xc, modv, g_mix, g_ffn, *branches, w_gate, w_branch, w_out, w_router, b_router)


def _slot_plan(ids):
    te = EXPERT_TILE
    flat = ids.reshape(-1)
    n_pairs = flat.shape[0]
    n_tiles = n_pairs // te + N_EXPERTS
    onehot = (flat[:, None] == jnp.arange(N_EXPERTS, dtype=jnp.int32)[None, :]).astype(jnp.int32)
    running = jnp.cumsum(onehot, axis=0)
    counts = running[-1]
    rank = jnp.sum(running * onehot, axis=1) - 1
    padded = ((counts + te - 1) // te) * te
    ends = jnp.cumsum(padded)
    offsets = ends - padded
    dest = (jnp.sum(onehot * offsets[None, :], axis=1) + rank).astype(jnp.int32)
    tile_start = jnp.arange(n_tiles, dtype=jnp.int32) * te
    tile_expert = jnp.minimum(jnp.sum((tile_start[:, None] >= ends[None, :]).astype(jnp.int32), axis=1),
                              N_EXPERTS - 1).astype(jnp.int32)
    n_used = (ends[-1] // te).astype(jnp.int32).reshape(1)
    return dest, tile_expert, n_used, n_tiles


def _row_copy(src_ref, src_row, dst_ref, dst_row, sem):
    return pltpu.make_async_copy(src_ref.at[pl.ds(src_row, 1), :], dst_ref.at[pl.ds(dst_row, 1), :], sem)


def _dispatch_kernel(dest_ref, h_ref, init_ref, out_ref, sem):
    del init_ref
    tm = h_ref.shape[0]

    def start(r, carry):
        for k in range(2):
            _row_copy(h_ref, r, out_ref, dest_ref[0, 0, 2 * r + k], sem).start()
        return carry

    lax.fori_loop(0, tm, start, 0)

    def wait(r, carry):
        for k in range(2):
            _row_copy(h_ref, 0, out_ref, 0, sem).wait()
        return carry

    lax.fori_loop(0, tm, wait, 0)


def _dispatch(dest, h2, n_slots):
    n_tok, d = h2.shape
    tm = TOKEN_TILE
    return pl.pallas_call(
        _dispatch_kernel,
        grid=(n_tok // tm,),
        in_specs=[pl.BlockSpec((1, 1, 2 * tm), lambda i: (i, 0, 0), memory_space=pltpu.SMEM),
                  pl.BlockSpec((tm, d), lambda i: (i, 0)),
                  pl.BlockSpec(memory_space=pl.ANY)],
        out_specs=pl.BlockSpec(memory_space=pl.ANY),
        out_shape=jax.ShapeDtypeStruct((n_slots, d), F32),
        scratch_shapes=[pltpu.SemaphoreType.DMA(())],
        input_output_aliases={2: 0},
        compiler_params=_params("arbitrary"),
        name="moe_dispatch",
    )(dest.reshape(n_tok // tm, 1, 2 * tm), h2, jnp.zeros((n_slots, d), F32))


def _ffn_kernel(te_ref, used_ref, x_ref, wg_ref, wu_ref, wd_ref, o_ref):
    del te_ref
    i = pl.program_id(0)

    @pl.when(i < used_ref[0])
    def _():
        x = x_ref[...].astype(BF16)
        g = jnp.dot(x, wg_ref[0], preferred_element_type=F32)
        u = jnp.dot(x, wu_ref[0], preferred_element_type=F32)
        a = (g * jax.nn.sigmoid(g) * u).astype(BF16)
        o_ref[...] = jnp.dot(a, wd_ref[0], preferred_element_type=F32)

    @pl.when(i >= used_ref[0])
    def _():
        o_ref[...] = jnp.zeros_like(o_ref)


def _expert_ffn(tile_expert, n_used, xs, w_gate, w_up, w_down, n_tiles):
    n_slots, d = xs.shape
    te = EXPERT_TILE
    de = w_gate.shape[-1]
    grid_spec = pltpu.PrefetchScalarGridSpec(
        num_scalar_prefetch=2,
        grid=(n_tiles,),
        in_specs=[pl.BlockSpec((te, d), lambda i, te_ref, used: (i, 0)),
                  pl.BlockSpec((1, d, de), lambda i, te_ref, used: (te_ref[i], 0, 0)),
                  pl.BlockSpec((1, d, de), lambda i, te_ref, used: (te_ref[i], 0, 0)),
                  pl.BlockSpec((1, de, d), lambda i, te_ref, used: (te_ref[i], 0, 0))],
        out_specs=pl.BlockSpec((te, d), lambda i, te_ref, used: (i, 0)),
    )
    return pl.pallas_call(
        _ffn_kernel,
        grid_spec=grid_spec,
        out_shape=jax.ShapeDtypeStruct((n_slots, d), F32),
        compiler_params=_params("arbitrary"),
        name="moe_ffn",
    )(tile_expert, n_used, xs, w_gate, w_up, w_down)


def _combine_kernel(dest_ref, wts_ref, x1_ref, mod_ref, gf_ref, ys_ref, o_ref, buf, sem, *, final):
    tm = x1_ref.shape[1]

    def start(r, carry):
        for k in range(2):
            _row_copy(ys_ref, dest_ref[0, 0, 2 * r + k], buf.at[k], r, sem).start()
        return carry

    lax.fori_loop(0, tm, start, 0)

    def wait(r, carry):
        for k in range(2):
            _row_copy(ys_ref, 0, buf.at[k], 0, sem).wait()
        return carry

    lax.fori_loop(0, tm, wait, 0)
    wts = wts_ref[0]
    y = wts[:, 0:1] * buf[0] + wts[:, 1:2] * buf[1]
    x2 = x1_ref[0] + mod_ref[0, 0, 5:6, :] * y
    if final:
        ms = jnp.mean(x2 * x2, axis=-1, keepdims=True)
        x2 = x2 * lax.rsqrt(ms + RMS_EPS) * gf_ref[...]
    o_ref[0] = x2


def _combine(dest, wts, x1, modv, g_final, ys, n_lat, final):
    bsz, n_all, d = x1.shape
    tm = TOKEN_TILE
    lat_tiles = n_lat // tm
    all_tiles = n_all // tm
    tiles = lat_tiles if final else all_tiles
    n_out = n_lat if final else n_all
    return pl.pallas_call(
        functools.partial(_combine_kernel, final=final),
        grid=(bsz, tiles),
        in_specs=[pl.BlockSpec((1, 1, 2 * tm), lambda b, t: (b * all_tiles + t, 0, 0), memory_space=pltpu.SMEM),
                  pl.BlockSpec((1, tm, LANES), lambda b, t: (b, t, 0)),
                  pl.BlockSpec((1, tm, d), lambda b, t: (b, t, 0)),
                  pl.BlockSpec((1, 1, 6, d), lambda b, t: (b, jnp.minimum(t // lat_tiles, 1), 0, 0)),
                  pl.BlockSpec((1, d), lambda b, t: (0, 0)),
                  pl.BlockSpec(memory_space=pl.ANY)],
        out_specs=pl.BlockSpec((1, tm, d), lambda b, t: (b, t, 0)),
        out_shape=jax.ShapeDtypeStruct((bsz, n_out, d), F32),
        scratch_shapes=[pltpu.VMEM((2, tm, d), F32), pltpu.SemaphoreType.DMA(())],
        compiler_params=_params("arbitrary", "arbitrary"),
        name="moe_combine",
    )(dest.reshape(bsz * all_tiles, 1, 2 * tm), wts, x1, modv, g_final, ys)


def kernel(x, c, ctx, c_ctx, w_mod, b_mod, g_mix, g_ffn, w_in, pool_w, pool_scale, diff_lambda, diff_norm_g, nat_rpb, gqa_q_norm, gqa_k_norm, w_branch, w_out, w_router_group, b_router_group, w_router_expert, b_router_expert, w_exp_gate, w_exp_up, w_exp_down, g_final):
    bsz, n_lat, d = x.shape
    n_ctx = ctx.shape[1]
    n_all = n_lat + n_ctx
    depth = w_mod.shape[0]
    tm = TOKEN_TILE
    assert n_lat % tm == 0 and n_ctx % tm == 0 and n_lat % GRID_W == 0 and n_lat // GRID_W >= NAT_WIN_R
    assert d == w_in.shape[1] and w_in.shape[2] == PROJ_COLS + N_BRANCH * d

    xc = jnp.concatenate([x, ctx], axis=1)
    mod_rows = 16
    assert bsz + 1 <= mod_rows
    cc = jnp.zeros((mod_rows, d), F32).at[:bsz].set(c).at[bsz].set(c_ctx)
    mod = _modulation(cc, w_mod, b_mod)
    mod_lat = mod[:, :bsz].reshape(depth, bsz, 1, 6, d)
    mod_ctx = jnp.broadcast_to(mod[:, bsz].reshape(depth, 1, 1, 6, d), (depth, bsz, 1, 6, d))
    modv = jnp.concatenate([mod_lat, mod_ctx], axis=2)

    tabs = _rope_tables(n_lat, n_ctx, 256, DIFF_DQ) + _rope_tables(n_lat, n_ctx, 256, HEAD_DIM)
    head_of = np.arange(LANES) // HEAD_DIM
    ones_bd = jnp.asarray((head_of[:, None] == head_of[None, :]).astype(np.float32) / HEAD_DIM)
    group_of = np.arange(BRANCH_W) // POOL_GW
    pool_mask = jnp.asarray(group_of[:, None] == group_of[None, :])

    out = None
    for l in range(depth):
        final = l == depth - 1
        w_proj = w_in[l, :, :PROJ_COLS].astype(BF16)
        w_gate = w_in[l, :, PROJ_COLS:].astype(BF16)
        tile2 = lambda v: jnp.tile(v.reshape(1, HEAD_DIM), (1, LANES // HEAD_DIM))
        u, proj = _inproj(xc, modv[l], g_mix[l].reshape(1, d), w_proj, tabs, ones_bd,
                          tile2(gqa_q_norm[l]), tile2(gqa_k_norm[l]), n_lat)
        pool_bd = jnp.where(pool_mask, jnp.tile(pool_w[l].reshape(BRANCH_W, POOL_GW), (1, BRANCH_W // POOL_GW)),
                            0.0).astype(BF16)
        b_pool = _pool(u, pool_bd, pool_scale[l].reshape(1, BRANCH_W), n_lat)
        b_diff = _diff_attention(proj, diff_lambda[l], tile2(diff_norm_g[l]), ones_bd, n_lat, l)
        b_nat = _nat_attention(proj, _nat_bias(nat_rpb[l]), n_lat)
        b_gqa = _gqa_attention(proj, n_lat)
        n_route = N_GROUPS + N_EXPERTS
        w_router = jnp.zeros((d, LANES), F32).at[:, :N_GROUPS].set(w_router_group[l])
        w_router = w_router.at[:, N_GROUPS:n_route].set(w_router_expert[l])
        b_router = jnp.zeros((1, LANES), F32).at[0, :N_GROUPS].set(b_router_group[l])
        b_router = b_router.at[0, N_GROUPS:n_route].set(b_router_expert[l])
        x1, h2, ids, wts = _merge(xc, modv[l], g_mix[l].reshape(1, d), g_ffn[l].reshape(1, d),
                                  (b_pool, b_diff, b_nat, b_gqa), w_gate, w_branch[l].astype(BF16),
                                  w_out[l].astype(BF16), w_router, b_router, n_lat)
        dest, tile_expert, n_used, n_tiles = _slot_plan(ids[:, :, :2])
        xs = _dispatch(dest, h2.reshape(bsz * n_all, d), n_tiles * EXPERT_TILE)
        ys = _expert_ffn(tile_expert, n_used, xs, w_exp_gate[l].astype(BF16), w_exp_up[l].astype(BF16),
                         w_exp_down[l].astype(BF16), n_tiles)
        out = _combine(dest, wts, x1, modv[l], g_final.reshape(1, d), ys, n_lat, final)
        xc = out
    return out
```

```python
import functools
import math

import numpy as np
import jax
import jax.numpy as jnp
from jax import lax
from jax.experimental import pallas as pl
from jax.experimental.pallas import tpu as pltpu

GRID_W = 64
HEAD_DIM = 64
BRANCH_W = 256
N_BRANCH = 4
POOL_HALF_WINDOWS = (1, 2, 4, 8)
POOL_GW = 64
DIFF_DQ = 32
NAT_WIN_R = 8
NAT_WIN_C = 16
N_GROUPS = 4
EXPERTS_PER_GROUP = 8
N_EXPERTS = 32
D_EXPERT = 512
ROPE_THETA = 10000.0
RMS_EPS = 1e-6
NEG_INF = -1e30
LOG2E = 1.4426950408889634

LANES = 128
TOKEN_TILE = 256
EXPERT_TILE = 256
VMEM_LIMIT_BYTES = 56 * 1024 * 1024

PROJ_COLS = 2304
COL_DQ, COL_DK, COL_DV = 0, 256, 512
COL_NQ, COL_NK, COL_NV = 768, 1024, 1280
COL_GQ, COL_GK, COL_GV = 1536, 2048, 2176

F32 = jnp.float32
BF16 = jnp.bfloat16
HIGHEST = lax.Precision.HIGHEST


def _params(*sem):
    return pltpu.CompilerParams(dimension_semantics=sem, vmem_limit_bytes=VMEM_LIMIT_BYTES)


def _lane_iota(shape):
    return lax.broadcasted_iota(jnp.int32, shape, len(shape) - 1)


def _dot_nt(a, b):
    return lax.dot_general(a, b, (((1,), (1,)), ((), ())), preferred_element_type=F32)


def _modulated_norm(x, gain, shift, scale):
    ms = jnp.mean(x * x, axis=-1, keepdims=True)
    return (x * lax.rsqrt(ms + RMS_EPS)) * gain * (1.0 + scale) + shift


def _mod_kernel(c_ref, w_ref, b_ref, o_ref):
    c = c_ref[...]
    s = c * jax.nn.sigmoid(c)
    o_ref[0] = jnp.dot(s, w_ref[0], preferred_element_type=F32, precision=HIGHEST) + b_ref[0]


def _modulation(cc, w_mod, b_mod):
    depth, d, n6 = w_mod.shape
    rows = cc.shape[0]
    tn = 512
    return pl.pallas_call(
        _mod_kernel,
        grid=(depth, n6 // tn),
        in_specs=[pl.BlockSpec((rows, d), lambda l, j: (0, 0)),
                  pl.BlockSpec((1, d, tn), lambda l, j: (l, 0, j)),
                  pl.BlockSpec((1, 1, tn), lambda l, j: (l, 0, j))],
        out_specs=pl.BlockSpec((1, rows, tn), lambda l, j: (l, 0, j)),
        out_shape=jax.ShapeDtypeStruct((depth, rows, n6), F32),
        compiler_params=_params("arbitrary", "arbitrary"),
        name="modulation",
    )(cc, w_mod, b_mod.reshape(depth, 1, n6))


def _rope_tables(n_lat, n_ctx, width, vec_dim):
    t = np.arange(n_lat)
    row, col = t // GRID_W, t % GRID_W
    j = np.arange(width) % vec_dim
    half = vec_dim // 2
    quarter = half // 2
    jj = j % half
    inv = ROPE_THETA ** (-(jj % quarter).astype(np.float64) / quarter)
    pos = np.where((j < half)[None, :], row[:, None], col[:, None]).astype(np.float64)
    ang = pos * inv[None, :]
    sign = np.where(jj < quarter, -1.0, 1.0)[None, :]
    cos = np.concatenate([np.cos(ang), np.ones((n_ctx, width))], axis=0)
    sin = np.concatenate([np.sin(ang) * sign, np.zeros((n_ctx, width))], axis=0)
    return jnp.asarray(cos, F32), jnp.asarray(sin, F32)


def _rope(x, cos, sin, quarter):
    lane = _lane_iota(x.shape)
    first = (lane % (2 * quarter)) < quarter
    up = pltpu.roll(x, LANES - quarter, 1)
    down = pltpu.roll(x, quarter, 1)
    return x * cos + jnp.where(first, up, down) * sin


def _head_rms(x, ones_bd, gain):
    ms = jnp.dot(x * x, ones_bd, preferred_element_type=F32, precision=HIGHEST)
    return x * lax.rsqrt(ms + RMS_EPS) * gain


def _inproj_kernel(x_ref, mod_ref, g_ref, w_ref, cd_ref, sd_ref, cg_ref, sg_ref, ones_ref, qn_ref, kn_ref,
                   u_ref, p_ref):
    x = x_ref[0]
    h = _modulated_norm(x, g_ref[...], mod_ref[0, 0, 0:1, :], mod_ref[0, 0, 1:2, :]).astype(BF16)

    def proj(col, width=LANES):
        return jnp.dot(h, w_ref[:, col:col + width], preferred_element_type=F32)

    for b in range(2):
        u_ref[0, :, b * LANES:(b + 1) * LANES] = proj(b * LANES)
    for dst, src, mult in ((COL_DQ, 256, DIFF_DQ ** -0.5 * LOG2E), (COL_DK, 512, 1.0)):
        for b in range(2):
            sl = slice(b * LANES, (b + 1) * LANES)
            y = _rope(proj(src + b * LANES), cd_ref[:, sl], sd_ref[:, sl], DIFF_DQ // 4) * mult
            p_ref[0, :, dst + b * LANES:dst + (b + 1) * LANES] = y.astype(BF16)
    for dst, src, width in ((COL_DV, 768, 256), (COL_NQ, 1024, 256), (COL_NK, 1280, 256), (COL_NV, 1536, 256),
                            (COL_GV, 2176, 128)):
        for b in range(width // LANES):
            p_ref[0, :, dst + b * LANES:dst + (b + 1) * LANES] = proj(src + b * LANES).astype(BF16)
    ones_bd = ones_ref[...]
    lane = _lane_iota((x.shape[0], LANES))
    low = lane < HEAD_DIM
    for b in range(2):
        sl = slice(b * LANES, (b + 1) * LANES)
        y = _head_rms(proj(1792 + b * LANES), ones_bd, qn_ref[...])
        y = _rope(y, cg_ref[:, sl], sg_ref[:, sl], HEAD_DIM // 4) * (HEAD_DIM ** -0.5 * LOG2E)
        swapped = pltpu.roll(y, HEAD_DIM, 1)
        keep = low if b == 0 else jnp.logical_not(low)
        even = jnp.where(keep, y if b == 0 else swapped, 0.0)
        odd = jnp.where(keep, swapped if b == 0 else y, 0.0)
        p_ref[0, :, COL_GQ + (2 * b) * LANES:COL_GQ + (2 * b + 1) * LANES] = even.astype(BF16)
        p_ref[0, :, COL_GQ + (2 * b + 1) * LANES:COL_GQ + (2 * b + 2) * LANES] = odd.astype(BF16)
    y = _head_rms(proj(2048), ones_bd, kn_ref[...])
    y = _rope(y, cg_ref[:, 0:LANES], sg_ref[:, 0:LANES], HEAD_DIM // 4)
    p_ref[0, :, COL_GK:COL_GK + LANES] = y.astype(BF16)


def _inproj(xc, modv, g_mix, w_proj, tabs, ones_bd, q_norm, k_norm, n_lat):
    bsz, n_all, d = xc.shape
    tm = TOKEN_TILE
    lat_tiles = n_lat // tm
    cd, sd, cg, sg = tabs
    full = lambda shape: pl.BlockSpec(shape, lambda b, t: (0,) * len(shape))
    tab = pl.BlockSpec((tm, 256), lambda b, t: (t, 0))
    return pl.pallas_call(
        _inproj_kernel,
        grid=(bsz, n_all // tm),
        in_specs=[pl.BlockSpec((1, tm, d), lambda b, t: (b, t, 0)),
                  pl.BlockSpec((1, 1, 6, d), lambda b, t: (b, jnp.minimum(t // lat_tiles, 1), 0, 0)),
                  full((1, d)), full((d, PROJ_COLS)), tab, tab, tab, tab,
                  full((LANES, LANES)), full((1, LANES)), full((1, LANES))],
        out_specs=[pl.BlockSpec((1, tm, BRANCH_W), lambda b, t: (b, t, 0)),
                   pl.BlockSpec((1, tm, PROJ_COLS), lambda b, t: (b, t, 0))],
        out_shape=[jax.ShapeDtypeStruct((bsz, n_all, BRANCH_W), F32),
                   jax.ShapeDtypeStruct((bsz, n_all, PROJ_COLS), BF16)],
        compiler_params=_params("parallel", "parallel"),
        name="inproj",
    )(xc, modv, g_mix, w_proj, cd, sd, cg, sg, ones_bd, q_norm, k_norm)


def _pool_kernel(prev_ref, u_ref, next_ref, w_ref, scale_ref, o_ref, pad_ref, *, n_lat, n_all):
    t = pl.program_id(1)
    tm = TOKEN_TILE
    lat_tiles = n_lat // tm
    all_tiles = n_all // tm
    halo = 8
    u = u_ref[0]
    first = jnp.logical_or(t == 0, t == lat_tiles)
    last = jnp.logical_or(t == lat_tiles - 1, t == all_tiles - 1)
    pad_ref[0:halo, :] = jnp.where(first, 0.0, prev_ref[0])
    pad_ref[halo:halo + tm, :] = u
    pad_ref[halo + tm:2 * halo + tm, :] = jnp.where(last, 0.0, next_ref[0])

    def shifted(d):
        return pad_ref[halo + d:halo + d + tm, :]

    sums = []
    acc = None
    for hw in POOL_HALF_WINDOWS:
        lo = hw // 2 if acc is not None else 0
        for d in range(lo, hw):
            term = shifted(d) + shifted(-d - 1)
            acc = term if acc is None else acc + term
        sums.append(acc)
    lane = _lane_iota((tm, BRANCH_W))
    group = lane // POOL_GW
    win = jnp.where(group == 0, sums[0], jnp.where(group == 1, sums[1], jnp.where(group == 2, sums[2], sums[3])))
    half = jnp.where(group == 0, 1, jnp.where(group == 1, 2, jnp.where(group == 2, 4, 8)))
    seg_start = jnp.where(t >= lat_tiles, n_lat, 0)
    seg_len = jnp.where(t >= lat_tiles, n_all - n_lat, n_lat)
    pos = t * tm - seg_start + lax.broadcasted_iota(jnp.int32, (tm, BRANCH_W), 0)
    cnt = jnp.minimum(pos + half, seg_len) - jnp.maximum(pos - half, 0)
    pooled = (win / cnt.astype(F32) - u).astype(BF16)
    y = jnp.dot(pooled, w_ref[...], preferred_element_type=F32) * scale_ref[...]
    o_ref[0] = y.astype(BF16)


def _pool(u, w_bd, pool_scale, n_lat):
    bsz, n_all, ch = u.shape
    tm = TOKEN_TILE
    halo = 8
    per = tm // halo
    nblk = n_all // halo
    return pl.pallas_call(
        functools.partial(_pool_kernel, n_lat=n_lat, n_all=n_all),
        grid=(bsz, n_all // tm),
        in_specs=[pl.BlockSpec((1, halo, ch), lambda b, t: (b, jnp.maximum(t * per - 1, 0), 0)),
                  pl.BlockSpec((1, tm, ch), lambda b, t: (b, t, 0)),
                  pl.BlockSpec((1, halo, ch), lambda b, t: (b, jnp.minimum((t + 1) * per, nblk - 1), 0)),
                  pl.BlockSpec((ch, ch), lambda b, t: (0, 0)),
                  pl.BlockSpec((1, ch), lambda b, t: (0, 0))],
        out_specs=pl.BlockSpec((1, tm, ch), lambda b, t: (b, t, 0)),
        out_shape=jax.ShapeDtypeStruct((bsz, n_all, ch), BF16),
        scratch_shapes=[pltpu.VMEM((tm + 2 * halo, ch), F32)],
        compiler_params=_params("parallel", "parallel"),
        name="pool",
    )(u, u, u, w_bd, pool_scale)


ATTN_Q_TILE = 256
ATTN_K_TILE = 512


ATTN_ROW_BLOCK = 64


def _flash_scratch(rows, n_ctx):
    wide = lambda: pltpu.VMEM((rows, LANES), F32)
    return [pltpu.VMEM((rows, LANES), BF16), pltpu.VMEM((2, rows, ATTN_K_TILE), F32),
            pltpu.VMEM((rows, n_ctx), F32), pltpu.VMEM((rows, ATTN_K_TILE), BF16), wide(), wide(), wide(), wide()]


def _flash_attention(k_ref, v_ref, scratch, is_latent_block, n_lat, n_all):
    q_scr, s_scr, sc_scr, p_scr, m_scr, l_scr, a_scr, acc_scr = scratch
    rows = q_scr.shape[0]
    tk = ATTN_K_TILE
    rb = ATTN_ROW_BLOCK
    n_chunks = n_lat // tk
    n_ctx = n_all - n_lat
    assert n_chunks % 2 == 0 and n_chunks * tk == n_lat and n_ctx <= tk

    m_scr[...] = jnp.full(m_scr.shape, -jnp.inf, F32)
    l_scr[...] = jnp.zeros(l_scr.shape, F32)
    acc_scr[...] = jnp.zeros(acc_scr.shape, F32)

    def produce(dst, start, size):
        dst[...] = _dot_nt(q_scr[...], k_ref[0, pl.ds(start, size), :])

    def consume(src, start, size):
        for r in range(rows // rb):
            rs = slice(r * rb, (r + 1) * rb)
            s = src[rs, :]
            m_old = m_scr[rs, 0:1]
            m_new = jnp.maximum(m_old, jnp.max(s, axis=-1, keepdims=True))
            alpha = jnp.exp2(m_old - m_new)
            p = jnp.exp2(s - m_new)
            p_scr[rs, 0:size] = p.astype(BF16)
            part = p[:, 0:LANES]
            for b in range(1, size // LANES):
                part = part + p[:, b * LANES:(b + 1) * LANES]
            l_scr[rs, :] = alpha * l_scr[rs, :] + part
            m_scr[rs, :] = jnp.broadcast_to(m_new, (rb, LANES))
            a_scr[rs, :] = jnp.broadcast_to(alpha, (rb, LANES))
        pv = jnp.dot(p_scr[:, 0:size], v_ref[0, pl.ds(start, size), :], preferred_element_type=F32)
        acc_scr[...] = a_scr[...] * acc_scr[...] + pv

    def lat(chunk):
        return pl.multiple_of(chunk * tk, tk)

    produce(sc_scr, n_lat, n_ctx)

    @pl.when(is_latent_block)
    def _():
        produce(s_scr.at[0], 0, tk)

        def step(i, carry):
            produce(s_scr.at[1], lat(2 * i + 1), tk)
            consume(s_scr.at[0], lat(2 * i), tk)
            produce(s_scr.at[0], lat(2 * i + 2), tk)
            consume(s_scr.at[1], lat(2 * i + 1), tk)
            return carry

        lax.fori_loop(0, n_chunks // 2 - 1, step, 0)
        produce(s_scr.at[1], (n_chunks - 1) * tk, tk)
        consume(s_scr.at[0], (n_chunks - 2) * tk, tk)
        consume(s_scr.at[1], (n_chunks - 1) * tk, tk)

    consume(sc_scr, n_lat, n_ctx)
    return acc_scr[...] / jnp.sum(l_scr[...], axis=-1, keepdims=True)


def _diff_kernel(lam_ref, g_ref, ones_ref, q_ref, k_ref, v_ref, o_ref, *scratch, n_lat, n_all, tq, lam_init):
    i = pl.program_id(2)
    lf = lam_ref[...]
    lam = (jnp.exp(jnp.sum(lf[0:1] * lf[1:2], axis=-1, keepdims=True))
           - jnp.exp(jnp.sum(lf[2:3] * lf[3:4], axis=-1, keepdims=True)) + lam_init)
    q = q_ref[0]
    lane = _lane_iota((tq, LANES))
    q_scr = scratch[0]
    for n in range(LANES // DIFF_DQ):
        start = n * DIFF_DQ
        mine = jnp.logical_and(lane >= start, lane < start + DIFF_DQ)
        q_scr[n * tq:(n + 1) * tq, :] = jnp.where(mine, q, jnp.zeros_like(q))
    out = _flash_attention(k_ref, v_ref, scratch, i * tq < n_lat, n_lat, n_all)
    o = jnp.where(lane < HEAD_DIM, out[0:tq] - lam * out[tq:2 * tq], out[2 * tq:3 * tq] - lam * out[3 * tq:4 * tq])
    ms = jnp.dot(o * o, ones_ref[...], preferred_element_type=F32, precision=HIGHEST)
    o_ref[0] = (o * lax.rsqrt(ms + RMS_EPS) * g_ref[...] * (1.0 - lam_init)).astype(BF16)


def _diff_attention(proj, lam_p, norm_g, ones_bd, n_lat, layer):
    bsz, n_all, _ = proj.shape
    tq = ATTN_Q_TILE
    lam_init = 0.8 - 0.6 * math.exp(-0.3 * layer)
    qb, kb, vb = COL_DQ // LANES, COL_DK // LANES, COL_DV // LANES
    return pl.pallas_call(
        functools.partial(_diff_kernel, n_lat=n_lat, n_all=n_all, tq=tq, lam_init=lam_init),
        grid=(bsz, 2, n_all // tq),
        in_specs=[pl.BlockSpec(lam_p.shape, lambda b, p, i: (0, 0)),
                  pl.BlockSpec((1, LANES), lambda b, p, i: (0, 0)),
                  pl.BlockSpec((LANES, LANES), lambda b, p, i: (0, 0)),
                  pl.BlockSpec((1, tq, LANES), lambda b, p, i: (b, i, qb + p)),
                  pl.BlockSpec((1, n_all, LANES), lambda b, p, i: (b, 0, kb + p)),
                  pl.BlockSpec((1, n_all, LANES), lambda b, p, i: (b, 0, vb + p))],
        out_specs=pl.BlockSpec((1, tq, LANES), lambda b, p, i: (b, i, p)),
        out_shape=jax.ShapeDtypeStruct((bsz, n_all, BRANCH_W), BF16),
        scratch_shapes=_flash_scratch(4 * tq, n_all - n_lat),
        compiler_params=_params("parallel", "parallel", "parallel"),
        name="diff_attention",
    )(lam_p, norm_g, ones_bd, proj, proj, proj)


def _gqa_kernel(q_ref, k_ref, v_ref, o_ref, *scratch, n_lat, n_all, tq):
    kv = pl.program_id(1)
    i = pl.program_id(2)
    q_scr = scratch[0]
    for g in range(2):
        q_scr[g * tq:(g + 1) * tq, :] = q_ref[0, :, g * LANES:(g + 1) * LANES]
    out = _flash_attention(k_ref, v_ref, scratch, i * tq < n_lat, n_lat, n_all)
    outs = [out[0:tq], out[tq:2 * tq]]
    placed = [jnp.where(kv == g, outs[g], pltpu.roll(outs[g], HEAD_DIM, 1)) for g in range(2)]
    lane = _lane_iota((tq, LANES))
    o_ref[0] = jnp.where(lane < HEAD_DIM, placed[0], placed[1]).astype(BF16)


def _gqa_attention(proj, n_lat):
    bsz, n_all, _ = proj.shape
    tq = ATTN_Q_TILE
    qb, kb, vb = COL_GQ // (2 * LANES), COL_GK // LANES, COL_GV // LANES
    return pl.pallas_call(
        functools.partial(_gqa_kernel, n_lat=n_lat, n_all=n_all, tq=tq),
        grid=(bsz, 2, n_all // tq),
        in_specs=[pl.BlockSpec((1, tq, 2 * LANES), lambda b, p, i: (b, i, qb + p)),
                  pl.BlockSpec((1, n_all, LANES), lambda b, p, i: (b, 0, kb)),
                  pl.BlockSpec((1, n_all, LANES), lambda b, p, i: (b, 0, vb))],
        out_specs=pl.BlockSpec((1, tq, LANES), lambda b, p, i: (b, i, p)),
        out_shape=jax.ShapeDtypeStruct((bsz, n_all, BRANCH_W), BF16),
        scratch_shapes=_flash_scratch(2 * tq, n_all - n_lat),
        compiler_params=_params("parallel", "parallel", "parallel"),
        name="gqa_attention",
    )(proj, proj, proj)


def _nat_bias(rpb):
    kc = NAT_WIN_C
    j = np.arange(GRID_W)
    col_start = np.clip(j - kc // 2, 0, GRID_W - kc)
    valid = (j[None, :] >= col_start[:, None]) & (j[None, :] < col_start[:, None] + kc)
    dc = np.clip(j[None, :] - j[:, None], -(kc - 1), kc - 1) + (kc - 1)
    tab = rpb[:, :, dc]
    tab = jnp.where(jnp.asarray(valid)[None, None], tab.astype(F32), NEG_INF)
    heads = rpb.shape[0]
    slabs = []
    for dr0 in range(NAT_WIN_R):
        blk = tab[:, dr0:dr0 + NAT_WIN_R]
        slabs.append(blk.transpose(0, 2, 1, 3).reshape(heads * GRID_W, NAT_WIN_R * GRID_W))
    return jnp.stack(slabs)


def _nat_kernel(bias_ref, q_ref, k_ref, v_ref, o_ref, *, n_lat, n_all):
    w = GRID_W
    rows = n_lat // w
    heads = BRANCH_W // HEAD_DIM
    scale = HEAD_DIM ** -0.5
    lane = _lane_iota((w, BRANCH_W))
    head_of_lane = lane // HEAD_DIM
    k_ctx = k_ref[0, n_lat:n_all, :]
    v_ctx = v_ref[0, n_lat:n_all, :]

    def stacked_queries(start):
        q = q_ref[0, pl.ds(start, w), :]
        return jnp.concatenate([jnp.where(head_of_lane == h, q, jnp.zeros_like(q)) for h in range(heads)], axis=0)

    def unstack(o):
        out = jnp.zeros((w, BRANCH_W), F32)
        for h in range(heads):
            out = jnp.where(head_of_lane == h, o[h * w:(h + 1) * w, :], out)
        return out

    def grid_row(r, carry):
        r0 = jnp.clip(r - NAT_WIN_R // 2, 0, rows - NAT_WIN_R)
        dr0 = r0 - r + (NAT_WIN_R - 1)
        qs = stacked_queries(pl.multiple_of(r * w, w))
        kb = k_ref[0, pl.ds(pl.multiple_of(r0 * w, w), NAT_WIN_R * w), :]
        vb = v_ref[0, pl.ds(pl.multiple_of(r0 * w, w), NAT_WIN_R * w), :]
        s_band = _dot_nt(qs, kb) * scale + bias_ref[dr0]
        s_ctx = _dot_nt(qs, k_ctx) * scale
        m = jnp.maximum(jnp.max(s_band, axis=-1, keepdims=True), jnp.max(s_ctx, axis=-1, keepdims=True))
        p_band = jnp.exp(s_band - m)
        p_ctx = jnp.exp(s_ctx - m)
        total = jnp.sum(p_band, axis=-1, keepdims=True) + jnp.sum(p_ctx, axis=-1, keepdims=True)
        o = (jnp.dot(p_band.astype(BF16), vb, preferred_element_type=F32)
             + jnp.dot(p_ctx.astype(BF16), v_ctx, preferred_element_type=F32)) / total
        o_ref[0, pl.ds(pl.multiple_of(r * w, w), w), :] = unstack(o).astype(BF16)
        return carry

    lax.fori_loop(0, rows, grid_row, 0)

    def ctx_block(cb, carry):
        start = pl.multiple_of(n_lat + cb * w, w)
        qs = stacked_queries(start)
        s = _dot_nt(qs, k_ctx) * scale
        p = jnp.exp(s - jnp.max(s, axis=-1, keepdims=True))
        o = jnp.dot(p.astype(BF16), v_ctx, preferred_element_type=F32) / jnp.sum(p, axis=-1, keepdims=True)
        o_ref[0, pl.ds(start, w), :] = unstack(o).astype(BF16)
        return carry

    lax.fori_loop(0, (n_all - n_lat) // w, ctx_block, 0)


def _nat_attention(proj, bias, n_lat):
    bsz, n_all, _ = proj.shape
    qb, kb, vb = COL_NQ // BRANCH_W, COL_NK // BRANCH_W, COL_NV // BRANCH_W
    return pl.pallas_call(
        functools.partial(_nat_kernel, n_lat=n_lat, n_all=n_all),
        grid=(bsz,),
        in_specs=[pl.BlockSpec(bias.shape, lambda b: (0, 0, 0)),
                  pl.BlockSpec((1, n_all, BRANCH_W), lambda b: (b, 0, qb)),
                  pl.BlockSpec((1, n_all, BRANCH_W), lambda b: (b, 0, kb)),
                  pl.BlockSpec((1, n_all, BRANCH_W), lambda b: (b, 0, vb))],
        out_specs=pl.BlockSpec((1, n_all, BRANCH_W), lambda b: (b, 0, 0)),
        out_shape=jax.ShapeDtypeStruct((bsz, n_all, BRANCH_W), BF16),
        compiler_params=_params("parallel"),
        name="nat_attention",
    )(bias, proj, proj, proj)


def _route(logits):
    lane = _lane_iota(logits.shape)
    lane_f = lane.astype(F32)
    far = float(LANES)
    gl = jnp.where(lane < N_GROUPS, logits, -jnp.inf)
    gmax = jnp.max(gl, axis=-1, keepdims=True)
    grp = jnp.min(jnp.where(gl == gmax, lane_f, far), axis=-1, keepdims=True)
    w_grp = 1.0 / jnp.sum(jnp.exp(gl - gmax), axis=-1, keepdims=True)
    first = N_GROUPS + EXPERTS_PER_GROUP * grp
    el = jnp.where(jnp.logical_and(lane_f >= first, lane_f < first + EXPERTS_PER_GROUP), logits, -jnp.inf)
    v1 = jnp.max(el, axis=-1, keepdims=True)
    i1 = jnp.min(jnp.where(el == v1, lane_f, far), axis=-1, keepdims=True)
    el2 = jnp.where(lane_f == i1, -jnp.inf, el)
    v2 = jnp.max(el2, axis=-1, keepdims=True)
    i2 = jnp.min(jnp.where(el2 == v2, lane_f, far), axis=-1, keepdims=True)
    t = jnp.exp(v2 - v1)
    w1 = w_grp / (1.0 + t)
    w2 = w_grp * t / (1.0 + t)
    ids = jnp.where(lane == 0, i1 - N_GROUPS, jnp.where(lane == 1, i2 - N_GROUPS, 0.0)).astype(jnp.int32)
    wts = jnp.where(lane == 0, w1, jnp.where(lane == 1, w2, 0.0))
    return ids, wts


def _merge_kernel(x_ref, mod_ref, gm_ref, gf_ref, b0_ref, b1_ref, b2_ref, b3_ref, wg_ref, wb_ref, wo_ref,
                  wr_ref, br_ref, x1_ref, h2_ref, ids_ref, wts_ref):
    d = x_ref.shape[-1]
    x = x_ref[0]
    mod = lambda k: mod_ref[0, 0, k:k + 1, :]
    h = _modulated_norm(x, gm_ref[...], mod(0), mod(1)).astype(BF16)
    merged = None
    for i, b_ref in enumerate((b0_ref, b1_ref, b2_ref, b3_ref)):
        gate = jax.nn.sigmoid(jnp.dot(h, wg_ref[:, i * d:(i + 1) * d], preferred_element_type=F32))
        term = gate * jnp.dot(b_ref[0], wb_ref[i], preferred_element_type=F32)
        merged = term if merged is None else merged + term
    o = jnp.dot(merged.astype(BF16), wo_ref[...], preferred_element_type=F32)
    x1 = x + mod(2) * o
    x1_ref[0] = x1
    h2 = _modulated_norm(x1, gf_ref[...], mod(3), mod(4))
    h2_ref[0] = h2
    logits = jnp.dot(h2, wr_ref[...], preferred_element_type=F32, precision=HIGHEST) + br_ref[...]
    ids, wts = _route(logits)
    ids_ref[0] = ids
    wts_ref[0] = wts


def _merge(xc, modv, g_mix, g_ffn, branches, w_gate, w_branch, w_out, w_router, b_router, n_lat):
    bsz, n_all, d = xc.shape
    tm = TOKEN_TILE
    lat_tiles = n_lat // tm
    tile = lambda width: pl.BlockSpec((1, tm, width), lambda b, t: (b, t, 0))
    full = lambda shape: pl.BlockSpec(shape, lambda b, t: (0,) * len(shape))
    return pl.pallas_call(
        _merge_kernel,
        grid=(bsz, n_all // tm),
        in_specs=[tile(d), pl.BlockSpec((1, 1, 6, d), lambda b, t: (b, jnp.minimum(t // lat_tiles, 1), 0, 0)),
                  full((1, d)), full((1, d)),
                  tile(BRANCH_W), tile(BRANCH_W), tile(BRANCH_W), tile(BRANCH_W),
                  full(w_gate.shape), full(w_branch.shape), full(w_out.shape),
                  full(w_router.shape), full(b_router.shape)],
        out_specs=[tile(d), tile(d), tile(LANES), tile(LANES)],
        out_shape=[jax.ShapeDtypeStruct((bsz, n_all, d), F32), jax.ShapeDtypeStruct((bsz, n_all, d), F32),
                   jax.ShapeDtypeStruct((bsz, n_all, LANES), jnp.int32),
                   jax.ShapeDtypeStruct((bsz, n_all, LANES), F32)],
        compiler_params=_params("parallel", "parallel"),
        name="merge_router",
    )(xc, modv, g_mix, g_ffn, *branches, w_gate, w_branch, w_out, w_router, b_router)


def _slot_plan(ids):
    te = EXPERT_TILE
    flat = ids.reshape(-1)
    n_pairs = flat.shape[0]
    n_tiles = n_pairs // te + N_EXPERTS
    onehot = (flat[:, None] == jnp.arange(N_EXPERTS, dtype=jnp.int32)[None, :]).astype(jnp.int32)
    running = jnp.cumsum(onehot, axis=0)
    counts = running[-1]
    rank = jnp.sum(running * onehot, axis=1) - 1
    padded = ((counts + te - 1) // te) * te
    ends = jnp.cumsum(padded)
    offsets = ends - padded
    dest = (jnp.sum(onehot * offsets[None, :], axis=1) + rank).astype(jnp.int32)
    tile_start = jnp.arange(n_tiles, dtype=jnp.int32) * te
    tile_expert = jnp.minimum(jnp.sum((tile_start[:, None] >= ends[None, :]).astype(jnp.int32), axis=1),
                              N_EXPERTS - 1).astype(jnp.int32)
    n_used = (ends[-1] // te).astype(jnp.int32).reshape(1)
    return dest, tile_expert, n_used, n_tiles


def _row_copy(src_ref, src_row, dst_ref, dst_row, sem):
    return pltpu.make_async_copy(src_ref.at[pl.ds(src_row, 1), :], dst_ref.at[pl.ds(dst_row, 1), :], sem)


def _dispatch_kernel(dest_ref, h_ref, init_ref, out_ref, sem):
    del init_ref
    tm = h_ref.shape[0]

    def start(r, carry):
        for k in range(2):
            _row_copy(h_ref, r, out_ref, dest_ref[0, 0, 2 * r + k], sem).start()
        return carry

    lax.fori_loop(0, tm, start, 0)

    def wait(r, carry):
        for k in range(2):
            _row_copy(h_ref, 0, out_ref, 0, sem).wait()
        return carry

    lax.fori_loop(0, tm, wait, 0)


def _dispatch(dest, h2, n_slots):
    n_tok, d = h2.shape
    tm = TOKEN_TILE
    return pl.pallas_call(
        _dispatch_kernel,
        grid=(n_tok // tm,),
        in_specs=[pl.BlockSpec((1, 1, 2 * tm), lambda i: (i, 0, 0), memory_space=pltpu.SMEM),
                  pl.BlockSpec((tm, d), lambda i: (i, 0)),
                  pl.BlockSpec(memory_space=pl.ANY)],
        out_specs=pl.BlockSpec(memory_space=pl.ANY),
        out_shape=jax.ShapeDtypeStruct((n_slots, d), F32),
        scratch_shapes=[pltpu.SemaphoreType.DMA(())],
        input_output_aliases={2: 0},
        compiler_params=_params("arbitrary"),
        name="moe_dispatch",
    )(dest.reshape(n_tok // tm, 1, 2 * tm), h2, jnp.zeros((n_slots, d), F32))


def _ffn_kernel(te_ref, used_ref, x_ref, wg_ref, wu_ref, wd_ref, o_ref):
    del te_ref
    i = pl.program_id(0)

    @pl.when(i < used_ref[0])
    def _():
        x = x_ref[...].astype(BF16)
        g = jnp.dot(x, wg_ref[0], preferred_element_type=F32)
        u = jnp.dot(x, wu_ref[0], preferred_element_type=F32)
        a = (g * jax.nn.sigmoid(g) * u).astype(BF16)
        o_ref[...] = jnp.dot(a, wd_ref[0], preferred_element_type=F32)

    @pl.when(i >= used_ref[0])
    def _():
        o_ref[...] = jnp.zeros_like(o_ref)


def _expert_ffn(tile_expert, n_used, xs, w_gate, w_up, w_down, n_tiles):
    n_slots, d = xs.shape
    te = EXPERT_TILE
    de = w_gate.shape[-1]
    grid_spec = pltpu.PrefetchScalarGridSpec(
        num_scalar_prefetch=2,
        grid=(n_tiles,),
        in_specs=[pl.BlockSpec((te, d), lambda i, te_ref, used: (i, 0)),
                  pl.BlockSpec((1, d, de), lambda i, te_ref, used: (te_ref[i], 0, 0)),
                  pl.BlockSpec((1, d, de), lambda i, te_ref, used: (te_ref[i], 0, 0)),
                  pl.BlockSpec((1, de, d), lambda i, te_ref, used: (te_ref[i], 0, 0))],
        out_specs=pl.BlockSpec((te, d), lambda i, te_ref, used: (i, 0)),
    )
    return pl.pallas_call(
        _ffn_kernel,
        grid_spec=grid_spec,
        out_shape=jax.ShapeDtypeStruct((n_slots, d), F32),
        compiler_params=_params("arbitrary"),
        name="moe_ffn",
    )(tile_expert, n_used, xs, w_gate, w_up, w_down)


def _combine_kernel(dest_ref, wts_ref, x1_ref, mod_ref, gf_ref, ys_ref, o_ref, buf, sem, *, final):
    tm = x1_ref.shape[1]

    def start(r, carry):
        for k in range(2):
            _row_copy(ys_ref, dest_ref[0, 0, 2 * r + k], buf.at[k], r, sem).start()
        return carry

    lax.fori_loop(0, tm, start, 0)

    def wait(r, carry):
        for k in range(2):
            _row_copy(ys_ref, 0, buf.at[k], 0, sem).wait()
        return carry

    lax.fori_loop(0, tm, wait, 0)
    wts = wts_ref[0]
    y = wts[:, 0:1] * buf[0] + wts[:, 1:2] * buf[1]
    x2 = x1_ref[0] + mod_ref[0, 0, 5:6, :] * y
    if final:
        ms = jnp.mean(x2 * x2, axis=-1, keepdims=True)
        x2 = x2 * lax.rsqrt(ms + RMS_EPS) * gf_ref[...]
    o_ref[0] = x2


def _combine(dest, wts, x1, modv, g_final, ys, n_lat, final):
    bsz, n_all, d = x1.shape
    tm = TOKEN_TILE
    lat_tiles = n_lat // tm
    all_tiles = n_all // tm
    tiles = lat_tiles if final else all_tiles
    n_out = n_lat if final else n_all
    return pl.pallas_call(
        functools.partial(_combine_kernel, final=final),
        grid=(bsz, tiles),
        in_specs=[pl.BlockSpec((1, 1, 2 * tm), lambda b, t: (b * all_tiles + t, 0, 0), memory_space=pltpu.SMEM),
                  pl.BlockSpec((1, tm, LANES), lambda b, t: (b, t, 0)),
                  pl.BlockSpec((1, tm, d), lambda b, t: (b, t, 0)),
                  pl.BlockSpec((1, 1, 6, d), lambda b, t: (b, jnp.minimum(t // lat_tiles, 1), 0, 0)),
                  pl.BlockSpec((1, d), lambda b, t: (0, 0)),
                  pl.BlockSpec(memory_space=pl.ANY)],
        out_specs=pl.BlockSpec((1, tm, d), lambda b, t: (b, t, 0)),
        out_shape=jax.ShapeDtypeStruct((bsz, n_out, d), F32),
        scratch_shapes=[pltpu.VMEM((2, tm, d), F32), pltpu.SemaphoreType.DMA(())],
        compiler_params=_params("arbitrary", "arbitrary"),
        name="moe_combine",
    )(dest.reshape(bsz * all_tiles, 1, 2 * tm), wts, x1, modv, g_final, ys)


def kernel(x, c, ctx, c_ctx, w_mod, b_mod, g_mix, g_ffn, w_in, pool_w, pool_scale, diff_lambda, diff_norm_g, nat_rpb, gqa_q_norm, gqa_k_norm, w_branch, w_out, w_router_group, b_router_group, w_router_expert, b_router_expert, w_exp_gate, w_exp_up, w_exp_down, g_final):
    bsz, n_lat, d = x.shape
    n_ctx = ctx.shape[1]
    n_all = n_lat + n_ctx
    depth = w_mod.shape[0]
    tm = TOKEN_TILE
    assert n_lat % tm == 0 and n_ctx % tm == 0 and n_lat % GRID_W == 0 and n_lat // GRID_W >= NAT_WIN_R
    assert d == w_in.shape[1] and w_in.shape[2] == PROJ_COLS + N_BRANCH * d

    xc = jnp.concatenate([x, ctx], axis=1)
    mod_rows = 16
    assert bsz + 1 <= mod_rows
    cc = jnp.zeros((mod_rows, d), F32).at[:bsz].set(c).at[bsz].set(c_ctx)
    mod = _modulation(cc, w_mod, b_mod)
    mod_lat = mod[:, :bsz].reshape(depth, bsz, 1, 6, d)
    mod_ctx = jnp.broadcast_to(mod[:, bsz].reshape(depth, 1, 1, 6, d), (depth, bsz, 1, 6, d))
    modv = jnp.concatenate([mod_lat, mod_ctx], axis=2)

    tabs = _rope_tables(n_lat, n_ctx, 256, DIFF_DQ) + _rope_tables(n_lat, n_ctx, 256, HEAD_DIM)
    head_of = np.arange(LANES) // HEAD_DIM
    ones_bd = jnp.asarray((head_of[:, None] == head_of[None, :]).astype(np.float32) / HEAD_DIM)
    group_of = np.arange(BRANCH_W) // POOL_GW
    pool_mask = jnp.asarray(group_of[:, None] == group_of[None, :])

    out = None
    for l in range(depth):
        final = l == depth - 1
        w_proj = w_in[l, :, :PROJ_COLS].astype(BF16)
        w_gate = w_in[l, :, PROJ_COLS:].astype(BF16)
        tile2 = lambda v: jnp.tile(v.reshape(1, HEAD_DIM), (1, LANES // HEAD_DIM))
        u, proj = _inproj(xc, modv[l], g_mix[l].reshape(1, d), w_proj, tabs, ones_bd,
                          tile2(gqa_q_norm[l]), tile2(gqa_k_norm[l]), n_lat)
        pool_bd = jnp.where(pool_mask, jnp.tile(pool_w[l].reshape(BRANCH_W, POOL_GW), (1, BRANCH_W // POOL_GW)),
                            0.0).astype(BF16)
        b_pool = _pool(u, pool_bd, pool_scale[l].reshape(1, BRANCH_W), n_lat)
        b_diff = _diff_attention(proj, diff_lambda[l], tile2(diff_norm_g[l]), ones_bd, n_lat, l)
        b_nat = _nat_attention(proj, _nat_bias(nat_rpb[l]), n_lat)
        b_gqa = _gqa_attention(proj, n_lat)
        n_route = N_GROUPS + N_EXPERTS
        w_router = jnp.zeros((d, LANES), F32).at[:, :N_GROUPS].set(w_router_group[l])
        w_router = w_router.at[:, N_GROUPS:n_route].set(w_router_expert[l])
        b_router = jnp.zeros((1, LANES), F32).at[0, :N_GROUPS].set(b_router_group[l])
        b_router = b_router.at[0, N_GROUPS:n_route].set(b_router_expert[l])
        x1, h2, ids, wts = _merge(xc, modv[l], g_mix[l].reshape(1, d), g_ffn[l].reshape(1, d),
                                  (b_pool, b_diff, b_nat, b_gqa), w_gate, w_branch[l].astype(BF16),
                                  w_out[l].astype(BF16), w_router, b_router, n_lat)
        dest, tile_expert, n_used, n_tiles = _slot_plan(ids[:, :, :2])
        xs = _dispatch(dest, h2.reshape(bsz * n_all, d), n_tiles * EXPERT_TILE)
        ys = _expert_ffn(tile_expert, n_used, xs, w_exp_gate[l].astype(BF16), w_exp_up[l].astype(BF16),
                         w_exp_down[l].astype(BF16), n_tiles)
        out = _combine(dest, wts, x1, modv[l], g_final.reshape(1, d), ys, n_lat, final)
        xc = out
    return out
```

```python
import functools
import math

import numpy as np
import jax
import jax.numpy as jnp
from jax import lax
from jax.experimental import pallas as pl
from jax.experimental.pallas import tpu as pltpu

GRID_W = 64
HEAD_DIM = 64
BRANCH_W = 256
N_BRANCH = 4
POOL_HALF_WINDOWS = (1, 2, 4, 8)
POOL_GW = 64
DIFF_DQ = 32
NAT_WIN_R = 8
NAT_WIN_C = 16
N_GROUPS = 4
EXPERTS_PER_GROUP = 8
N_EXPERTS = 32
D_EXPERT = 512
ROPE_THETA = 10000.0
RMS_EPS = 1e-6
NEG_INF = -1e30
LOG2E = 1.4426950408889634

LANES = 128
TOKEN_TILE = 256
EXPERT_TILE = 256
VMEM_LIMIT_BYTES = 56 * 1024 * 1024

PROJ_COLS = 2304
COL_DQ, COL_DK, COL_DV = 0, 256, 512
COL_NQ, COL_NK, COL_NV = 768, 1024, 1280
COL_GQ, COL_GK, COL_GV = 1536, 2048, 2176

F32 = jnp.float32
BF16 = jnp.bfloat16
HIGHEST = lax.Precision.HIGHEST


def _params(*sem):
    return pltpu.CompilerParams(dimension_semantics=sem, vmem_limit_bytes=VMEM_LIMIT_BYTES)


def _lane_iota(shape):
    return lax.broadcasted_iota(jnp.int32, shape, len(shape) - 1)


def _dot_nt(a, b):
    return lax.dot_general(a, b, (((1,), (1,)), ((), ())), preferred_element_type=F32)


def _split_bf16(x):
    hi = x.astype(BF16)
    return hi, (x - hi.astype(F32)).astype(BF16)


def _dot_split(a, b_hi, b_lo=None):
    a_hi, a_lo = _split_bf16(a)
    out = jnp.dot(a_hi, b_hi, preferred_element_type=F32) + jnp.dot(a_lo, b_hi, preferred_element_type=F32)
    if b_lo is not None:
        out = out + jnp.dot(a_hi, b_lo, preferred_element_type=F32)
    return out


def _pack_halves(y):
    w = y.shape[1] // 2
    hi = pltpu.bitcast(y[:, :w].astype(BF16).astype(F32), jnp.int32)
    lo = pltpu.bitcast(y[:, w:].astype(BF16).astype(F32), jnp.int32)
    return jnp.bitwise_or(hi, lax.shift_right_logical(lo, 16))


def _unpack_halves(word):
    hi = pltpu.bitcast(jnp.bitwise_and(word, -65536), F32)
    lo = pltpu.bitcast(lax.shift_left(word, 16), F32)
    return hi, lo


def _modulated_norm(x, gain, shift, scale):
    ms = jnp.mean(x * x, axis=-1, keepdims=True)
    return (x * lax.rsqrt(ms + RMS_EPS)) * gain * (1.0 + scale) + shift


def _mod_kernel(c_ref, w_ref, b_ref, o_ref):
    c = c_ref[...]
    s = c * jax.nn.sigmoid(c)
    o_ref[0] = jnp.dot(s, w_ref[0], preferred_element_type=F32, precision=HIGHEST) + b_ref[0]


def _modulation(cc, w_mod, b_mod):
    depth, d, n6 = w_mod.shape
    rows = cc.shape[0]
    tn = 512
    return pl.pallas_call(
        _mod_kernel,
        grid=(depth, n6 // tn),
        in_specs=[pl.BlockSpec((rows, d), lambda l, j: (0, 0)),
                  pl.BlockSpec((1, d, tn), lambda l, j: (l, 0, j)),
                  pl.BlockSpec((1, 1, tn), lambda l, j: (l, 0, j))],
        out_specs=pl.BlockSpec((1, rows, tn), lambda l, j: (l, 0, j)),
        out_shape=jax.ShapeDtypeStruct((depth, rows, n6), F32),
        compiler_params=_params("arbitrary", "arbitrary"),
        name="modulation",
    )(cc, w_mod, b_mod.reshape(depth, 1, n6))


def _rope_tables(n_lat, n_ctx, width, vec_dim):
    t = np.arange(n_lat)
    row, col = t // GRID_W, t % GRID_W
    j = np.arange(width) % vec_dim
    half = vec_dim // 2
    quarter = half // 2
    jj = j % half
    inv = ROPE_THETA ** (-(jj % quarter).astype(np.float64) / quarter)
    pos = np.where((j < half)[None, :], row[:, None], col[:, None]).astype(np.float64)
    ang = pos * inv[None, :]
    sign = np.where(jj < quarter, -1.0, 1.0)[None, :]
    cos = np.concatenate([np.cos(ang), np.ones((n_ctx, width))], axis=0)
    sin = np.concatenate([np.sin(ang) * sign, np.zeros((n_ctx, width))], axis=0)
    return jnp.asarray(cos, F32), jnp.asarray(sin, F32)


def _rope(x, cos, sin, quarter):
    lane = _lane_iota(x.shape)
    first = (lane % (2 * quarter)) < quarter
    up = pltpu.roll(x, LANES - quarter, 1)
    down = pltpu.roll(x, quarter, 1)
    return x * cos + jnp.where(first, up, down) * sin


def _head_rms(x, ones_bd, gain):
    ms = _dot_split(x * x, ones_bd)
    return x * lax.rsqrt(ms + RMS_EPS) * gain


def _inproj_kernel(x_ref, mod_ref, g_ref, w_ref, cd_ref, sd_ref, cg_ref, sg_ref, ones_ref, qn_ref, kn_ref,
                   u_ref, p_ref):
    x = x_ref[0]
    h = _modulated_norm(x, g_ref[...], mod_ref[0, 0, 0:1, :], mod_ref[0, 0, 1:2, :]).astype(BF16)

    def proj(col, width=LANES):
        return jnp.dot(h, w_ref[:, col:col + width], preferred_element_type=F32)

    for b in range(2):
        u_ref[0, :, b * LANES:(b + 1) * LANES] = proj(b * LANES)
    for dst, src, mult in ((COL_DQ, 256, DIFF_DQ ** -0.5 * LOG2E), (COL_DK, 512, 1.0)):
        for b in range(2):
            sl = slice(b * LANES, (b + 1) * LANES)
            y = _rope(proj(src + b * LANES), cd_ref[:, sl], sd_ref[:, sl], DIFF_DQ // 4) * mult
            p_ref[0, :, dst + b * LANES:dst + (b + 1) * LANES] = y.astype(BF16)
    for dst, src, width in ((COL_DV, 768, 256), (COL_NQ, 1024, 256), (COL_NK, 1280, 256), (COL_NV, 1536, 256),
                            (COL_GV, 2176, 128)):
        for b in range(width // LANES):
            p_ref[0, :, dst + b * LANES:dst + (b + 1) * LANES] = proj(src + b * LANES).astype(BF16)
    ones_bd = ones_ref[...]
    lane = _lane_iota((x.shape[0], LANES))
    low = lane < HEAD_DIM
    for b in range(2):
        sl = slice(b * LANES, (b + 1) * LANES)
        y = _head_rms(proj(1792 + b * LANES), ones_bd, qn_ref[...])
        y = _rope(y, cg_ref[:, sl], sg_ref[:, sl], HEAD_DIM // 4) * (HEAD_DIM ** -0.5 * LOG2E)
        swapped = pltpu.roll(y, HEAD_DIM, 1)
        keep = low if b == 0 else jnp.logical_not(low)
        even = jnp.where(keep, y if b == 0 else swapped, 0.0)
        odd = jnp.where(keep, swapped if b == 0 else y, 0.0)
        p_ref[0, :, COL_GQ + (2 * b) * LANES:COL_GQ + (2 * b + 1) * LANES] = even.astype(BF16)
        p_ref[0, :, COL_GQ + (2 * b + 1) * LANES:COL_GQ + (2 * b + 2) * LANES] = odd.astype(BF16)
    y = _head_rms(proj(2048), ones_bd, kn_ref[...])
    y = _rope(y, cg_ref[:, 0:LANES], sg_ref[:, 0:LANES], HEAD_DIM // 4)
    p_ref[0, :, COL_GK:COL_GK + LANES] = y.astype(BF16)


def _inproj(xc, modv, g_mix, w_proj, tabs, ones_bd, q_norm, k_norm, n_lat):
    bsz, n_all, d = xc.shape
    tm = TOKEN_TILE
    lat_tiles = n_lat // tm
    cd, sd, cg, sg = tabs
    full = lambda shape: pl.BlockSpec(shape, lambda b, t: (0,) * len(shape))
    tab = pl.BlockSpec((tm, 256), lambda b, t: (t, 0))
    return pl.pallas_call(
        _inproj_kernel,
        grid=(bsz, n_all // tm),
        in_specs=[pl.BlockSpec((1, tm, d), lambda b, t: (b, t, 0)),
                  pl.BlockSpec((1, 1, 6, d), lambda b, t: (b, jnp.minimum(t // lat_tiles, 1), 0, 0)),
                  full((1, d)), full((d, PROJ_COLS)), tab, tab, tab, tab,
                  full((LANES, LANES)), full((1, LANES)), full((1, LANES))],
        out_specs=[pl.BlockSpec((1, tm, BRANCH_W), lambda b, t: (b, t, 0)),
                   pl.BlockSpec((1, tm, PROJ_COLS), lambda b, t: (b, t, 0))],
        out_shape=[jax.ShapeDtypeStruct((bsz, n_all, BRANCH_W), F32),
                   jax.ShapeDtypeStruct((bsz, n_all, PROJ_COLS), BF16)],
        compiler_params=_params("parallel", "parallel"),
        name="inproj",
    )(xc, modv, g_mix, w_proj, cd, sd, cg, sg, ones_bd, q_norm, k_norm)


def _pool_kernel(prev_ref, u_ref, next_ref, w_ref, scale_ref, o_ref, pad_ref, *, n_lat, n_all):
    t = pl.program_id(1)
    tm = TOKEN_TILE
    lat_tiles = n_lat // tm
    all_tiles = n_all // tm
    halo = 8
    u = u_ref[0]
    first = jnp.logical_or(t == 0, t == lat_tiles)
    last = jnp.logical_or(t == lat_tiles - 1, t == all_tiles - 1)
    pad_ref[0:halo, :] = jnp.where(first, 0.0, prev_ref[0])
    pad_ref[halo:halo + tm, :] = u
    pad_ref[halo + tm:2 * halo + tm, :] = jnp.where(last, 0.0, next_ref[0])

    def shifted(d):
        return pad_ref[halo + d:halo + d + tm, :]

    sums = []
    acc = None
    for hw in POOL_HALF_WINDOWS:
        lo = hw // 2 if acc is not None else 0
        for d in range(lo, hw):
            term = shifted(d) + shifted(-d - 1)
            acc = term if acc is None else acc + term
        sums.append(acc)
    lane = _lane_iota((tm, BRANCH_W))
    group = lane // POOL_GW
    win = jnp.where(group == 0, sums[0], jnp.where(group == 1, sums[1], jnp.where(group == 2, sums[2], sums[3])))
    half = jnp.where(group == 0, 1, jnp.where(group == 1, 2, jnp.where(group == 2, 4, 8)))
    seg_start = jnp.where(t >= lat_tiles, n_lat, 0)
    seg_len = jnp.where(t >= lat_tiles, n_all - n_lat, n_lat)
    pos = t * tm - seg_start + lax.broadcasted_iota(jnp.int32, (tm, BRANCH_W), 0)
    cnt = jnp.minimum(pos + half, seg_len) - jnp.maximum(pos - half, 0)
    pooled = (win / cnt.astype(F32) - u).astype(BF16)
    y = jnp.dot(pooled, w_ref[...], preferred_element_type=F32) * scale_ref[...]
    o_ref[0] = y.astype(BF16)


def _pool(u, w_bd, pool_scale, n_lat):
    bsz, n_all, ch = u.shape
    tm = TOKEN_TILE
    halo = 8
    per = tm // halo
    nblk = n_all // halo
    return pl.pallas_call(
        functools.partial(_pool_kernel, n_lat=n_lat, n_all=n_all),
        grid=(bsz, n_all // tm),
        in_specs=[pl.BlockSpec((1, halo, ch), lambda b, t: (b, jnp.maximum(t * per - 1, 0), 0)),
                  pl.BlockSpec((1, tm, ch), lambda b, t: (b, t, 0)),
                  pl.BlockSpec((1, halo, ch), lambda b, t: (b, jnp.minimum((t + 1) * per, nblk - 1), 0)),
                  pl.BlockSpec((ch, ch), lambda b, t: (0, 0)),
                  pl.BlockSpec((1, ch), lambda b, t: (0, 0))],
        out_specs=pl.BlockSpec((1, tm, ch), lambda b, t: (b, t, 0)),
        out_shape=jax.ShapeDtypeStruct((bsz, n_all, ch), BF16),
        scratch_shapes=[pltpu.VMEM((tm + 2 * halo, ch), F32)],
        compiler_params=_params("parallel", "parallel"),
        name="pool",
    )(u, u, u, w_bd, pool_scale)


ATTN_Q_TILE = 256
ATTN_K_TILE = 512


ATTN_ROW_BLOCK = 64


def _flash_scratch(rows, n_ctx):
    wide = lambda: pltpu.VMEM((rows, LANES), F32)
    return [pltpu.VMEM((rows, LANES), BF16), pltpu.VMEM((2, rows, ATTN_K_TILE), F32),
            pltpu.VMEM((rows, n_ctx), F32), pltpu.VMEM((rows, ATTN_K_TILE), BF16), wide(), wide(), wide(), wide()]


def _flash_attention(k_ref, v_ref, scratch, is_latent_block, n_lat, n_all):
    q_scr, s_scr, sc_scr, p_scr, m_scr, l_scr, a_scr, acc_scr = scratch
    rows = q_scr.shape[0]
    tk = ATTN_K_TILE
    rb = ATTN_ROW_BLOCK
    n_chunks = n_lat // tk
    n_ctx = n_all - n_lat
    assert n_chunks % 2 == 0 and n_chunks * tk == n_lat and n_ctx <= tk

    m_scr[...] = jnp.full(m_scr.shape, -jnp.inf, F32)
    l_scr[...] = jnp.zeros(l_scr.shape, F32)
    acc_scr[...] = jnp.zeros(acc_scr.shape, F32)

    def produce(dst, start, size):
        dst[...] = _dot_nt(q_scr[...], k_ref[0, pl.ds(start, size), :])

    def consume(src, start, size):
        for r in range(rows // rb):
            rs = slice(r * rb, (r + 1) * rb)
            s = src[rs, :]
            m_old = m_scr[rs, 0:1]
            m_new = jnp.maximum(m_old, jnp.max(s, axis=-1, keepdims=True))
            alpha = jnp.exp2(m_old - m_new)
            p = jnp.exp2(s - m_new)
            p_scr[rs, 0:size] = p.astype(BF16)
            part = p[:, 0:LANES]
            for b in range(1, size // LANES):
                part = part + p[:, b * LANES:(b + 1) * LANES]
            l_scr[rs, :] = alpha * l_scr[rs, :] + part
            m_scr[rs, :] = jnp.broadcast_to(m_new, (rb, LANES))
            a_scr[rs, :] = jnp.broadcast_to(alpha, (rb, LANES))
        pv = jnp.dot(p_scr[:, 0:size], v_ref[0, pl.ds(start, size), :], preferred_element_type=F32)
        acc_scr[...] = a_scr[...] * acc_scr[...] + pv

    def lat(chunk):
        return pl.multiple_of(chunk * tk, tk)

    produce(sc_scr, n_lat, n_ctx)

    @pl.when(is_latent_block)
    def _():
        produce(s_scr.at[0], 0, tk)

        def step(i, carry):
            produce(s_scr.at[1], lat(2 * i + 1), tk)
            consume(s_scr.at[0], lat(2 * i), tk)
            produce(s_scr.at[0], lat(2 * i + 2), tk)
            consume(s_scr.at[1], lat(2 * i + 1), tk)
            return carry

        lax.fori_loop(0, n_chunks // 2 - 1, step, 0)
        produce(s_scr.at[1], (n_chunks - 1) * tk, tk)
        consume(s_scr.at[0], (n_chunks - 2) * tk, tk)
        consume(s_scr.at[1], (n_chunks - 1) * tk, tk)

    consume(sc_scr, n_lat, n_ctx)
    return acc_scr[...] / jnp.sum(l_scr[...], axis=-1, keepdims=True)


def _diff_kernel(lam_ref, g_ref, ones_ref, q_ref, k_ref, v_ref, o_ref, *scratch, n_lat, n_all, tq, lam_init):
    i = pl.program_id(2)
    lf = lam_ref[...]
    lam = (jnp.exp(jnp.sum(lf[0:1] * lf[1:2], axis=-1, keepdims=True))
           - jnp.exp(jnp.sum(lf[2:3] * lf[3:4], axis=-1, keepdims=True)) + lam_init)
    q = q_ref[0]
    lane = _lane_iota((tq, LANES))
    q_scr = scratch[0]
    for n in range(LANES // DIFF_DQ):
        start = n * DIFF_DQ
        mine = jnp.logical_and(lane >= start, lane < start + DIFF_DQ)
        q_scr[n * tq:(n + 1) * tq, :] = jnp.where(mine, q, jnp.zeros_like(q))
    out = _flash_attention(k_ref, v_ref, scratch, i * tq < n_lat, n_lat, n_all)
    o = jnp.where(lane < HEAD_DIM, out[0:tq] - lam * out[tq:2 * tq], out[2 * tq:3 * tq] - lam * out[3 * tq:4 * tq])
    ms = _dot_split(o * o, ones_ref[...])
    o_ref[0] = (o * lax.rsqrt(ms + RMS_EPS) * g_ref[...] * (1.0 - lam_init)).astype(BF16)


def _diff_attention(proj, lam_p, norm_g, ones_bd, n_lat, layer):
    bsz, n_all, _ = proj.shape
    tq = ATTN_Q_TILE
    lam_init = 0.8 - 0.6 * math.exp(-0.3 * layer)
    qb, kb, vb = COL_DQ // LANES, COL_DK // LANES, COL_DV // LANES
    return pl.pallas_call(
        functools.partial(_diff_kernel, n_lat=n_lat, n_all=n_all, tq=tq, lam_init=lam_init),
        grid=(bsz, 2, n_all // tq),
        in_specs=[pl.BlockSpec(lam_p.shape, lambda b, p, i: (0, 0)),
                  pl.BlockSpec((1, LANES), lambda b, p, i: (0, 0)),
                  pl.BlockSpec((LANES, LANES), lambda b, p, i: (0, 0)),
                  pl.BlockSpec((1, tq, LANES), lambda b, p, i: (b, i, qb + p)),
                  pl.BlockSpec((1, n_all, LANES), lambda b, p, i: (b, 0, kb + p)),
                  pl.BlockSpec((1, n_all, LANES), lambda b, p, i: (b, 0, vb + p))],
        out_specs=pl.BlockSpec((1, tq, LANES), lambda b, p, i: (b, i, p)),
        out_shape=jax.ShapeDtypeStruct((bsz, n_all, BRANCH_W), BF16),
        scratch_shapes=_flash_scratch(4 * tq, n_all - n_lat),
        compiler_params=_params("parallel", "parallel", "parallel"),
        name="diff_attention",
    )(lam_p, norm_g, ones_bd, proj, proj, proj)


def _gqa_kernel(q_ref, k_ref, v_ref, o_ref, *scratch, n_lat, n_all, tq):
    kv = pl.program_id(1)
    i = pl.program_id(2)
    q_scr = scratch[0]
    for g in range(2):
        q_scr[g * tq:(g + 1) * tq, :] = q_ref[0, :, g * LANES:(g + 1) * LANES]
    out = _flash_attention(k_ref, v_ref, scratch, i * tq < n_lat, n_lat, n_all)
    outs = [out[0:tq], out[tq:2 * tq]]
    placed = [jnp.where(kv == g, outs[g], pltpu.roll(outs[g], HEAD_DIM, 1)) for g in range(2)]
    lane = _lane_iota((tq, LANES))
    o_ref[0] = jnp.where(lane < HEAD_DIM, placed[0], placed[1]).astype(BF16)


def _gqa_attention(proj, n_lat):
    bsz, n_all, _ = proj.shape
    tq = ATTN_Q_TILE
    qb, kb, vb = COL_GQ // (2 * LANES), COL_GK // LANES, COL_GV // LANES
    return pl.pallas_call(
        functools.partial(_gqa_kernel, n_lat=n_lat, n_all=n_all, tq=tq),
        grid=(bsz, 2, n_all // tq),
        in_specs=[pl.BlockSpec((1, tq, 2 * LANES), lambda b, p, i: (b, i, qb + p)),
                  pl.BlockSpec((1, n_all, LANES), lambda b, p, i: (b, 0, kb)),
                  pl.BlockSpec((1, n_all, LANES), lambda b, p, i: (b, 0, vb))],
        out_specs=pl.BlockSpec((1, tq, LANES), lambda b, p, i: (b, i, p)),
        out_shape=jax.ShapeDtypeStruct((bsz, n_all, BRANCH_W), BF16),
        scratch_shapes=_flash_scratch(2 * tq, n_all - n_lat),
        compiler_params=_params("parallel", "parallel", "parallel"),
        name="gqa_attention",
    )(proj, proj, proj)


def _nat_bias(rpb):
    kc = NAT_WIN_C
    j = np.arange(GRID_W)
    col_start = np.clip(j - kc // 2, 0, GRID_W - kc)
    valid = (j[None, :] >= col_start[:, None]) & (j[None, :] < col_start[:, None] + kc)
    dc = np.clip(j[None, :] - j[:, None], -(kc - 1), kc - 1) + (kc - 1)
    tab = rpb[:, :, dc]
    tab = jnp.where(jnp.asarray(valid)[None, None], tab.astype(F32), NEG_INF)
    heads = rpb.shape[0]
    slabs = []
    for dr0 in range(NAT_WIN_R):
        blk = tab[:, dr0:dr0 + NAT_WIN_R]
        slabs.append(blk.transpose(0, 2, 1, 3).reshape(heads * GRID_W, NAT_WIN_R * GRID_W))
    return jnp.stack(slabs)


NAT_ROWS_PER_STEP = 4


def _nat_kernel(bias_ref, q_ref, k_ref, v_ref, o_ref, *, n_lat, n_all):
    w = GRID_W
    rows = n_lat // w
    heads = BRANCH_W // HEAD_DIM
    scale = HEAD_DIM ** -0.5
    lane = _lane_iota((w, BRANCH_W))
    head_of_lane = lane // HEAD_DIM
    k_ctx = k_ref[0, n_lat:n_all, :]
    v_ctx = v_ref[0, n_lat:n_all, :]

    def stacked_queries(start):
        q = q_ref[0, pl.ds(start, w), :]
        return jnp.concatenate([jnp.where(head_of_lane == h, q, jnp.zeros_like(q)) for h in range(heads)], axis=0)

    def unstack(o):
        out = jnp.zeros((w, BRANCH_W), F32)
        for h in range(heads):
            out = jnp.where(head_of_lane == h, o[h * w:(h + 1) * w, :], out)
        return out

    def grid_row(r):
        r0 = jnp.clip(r - NAT_WIN_R // 2, 0, rows - NAT_WIN_R)
        dr0 = r0 - r + (NAT_WIN_R - 1)
        qs = stacked_queries(pl.multiple_of(r * w, w))
        kb = k_ref[0, pl.ds(pl.multiple_of(r0 * w, w), NAT_WIN_R * w), :]
        vb = v_ref[0, pl.ds(pl.multiple_of(r0 * w, w), NAT_WIN_R * w), :]
        s_band = _dot_nt(qs, kb) * scale + bias_ref[dr0]
        s_ctx = _dot_nt(qs, k_ctx) * scale
        m = jnp.maximum(jnp.max(s_band, axis=-1, keepdims=True), jnp.max(s_ctx, axis=-1, keepdims=True))
        p_band = jnp.exp(s_band - m)
        p_ctx = jnp.exp(s_ctx - m)
        total = jnp.sum(p_band, axis=-1, keepdims=True) + jnp.sum(p_ctx, axis=-1, keepdims=True)
        o = (jnp.dot(p_band.astype(BF16), vb, preferred_element_type=F32)
             + jnp.dot(p_ctx.astype(BF16), v_ctx, preferred_element_type=F32)) / total
        o_ref[0, pl.ds(pl.multiple_of(r * w, w), w), :] = unstack(o).astype(BF16)

    def row_group(g, carry):
        for u in range(NAT_ROWS_PER_STEP):
            grid_row(g * NAT_ROWS_PER_STEP + u)
        return carry

    assert rows % NAT_ROWS_PER_STEP == 0
    lax.fori_loop(0, rows // NAT_ROWS_PER_STEP, row_group, 0)

    def ctx_block(cb, carry):
        start = pl.multiple_of(n_lat + cb * w, w)
        qs = stacked_queries(start)
        s = _dot_nt(qs, k_ctx) * scale
        p = jnp.exp(s - jnp.max(s, axis=-1, keepdims=True))
        o = jnp.dot(p.astype(BF16), v_ctx, preferred_element_type=F32) / jnp.sum(p, axis=-1, keepdims=True)
        o_ref[0, pl.ds(start, w), :] = unstack(o).astype(BF16)
        return carry

    lax.fori_loop(0, (n_all - n_lat) // w, ctx_block, 0)


def _nat_attention(proj, bias, n_lat):
    bsz, n_all, _ = proj.shape
    qb, kb, vb = COL_NQ // BRANCH_W, COL_NK // BRANCH_W, COL_NV // BRANCH_W
    return pl.pallas_call(
        functools.partial(_nat_kernel, n_lat=n_lat, n_all=n_all),
        grid=(bsz,),
        in_specs=[pl.BlockSpec(bias.shape, lambda b: (0, 0, 0)),
                  pl.BlockSpec((1, n_all, BRANCH_W), lambda b: (b, 0, qb)),
                  pl.BlockSpec((1, n_all, BRANCH_W), lambda b: (b, 0, kb)),
                  pl.BlockSpec((1, n_all, BRANCH_W), lambda b: (b, 0, vb))],
        out_specs=pl.BlockSpec((1, n_all, BRANCH_W), lambda b: (b, 0, 0)),
        out_shape=jax.ShapeDtypeStruct((bsz, n_all, BRANCH_W), BF16),
        compiler_params=_params("parallel"),
        name="nat_attention",
    )(bias, proj, proj, proj)


def _route(logits):
    lane = _lane_iota(logits.shape)
    lane_f = lane.astype(F32)
    far = float(LANES)
    gl = jnp.where(lane < N_GROUPS, logits, -jnp.inf)
    gmax = jnp.max(gl, axis=-1, keepdims=True)
    grp = jnp.min(jnp.where(gl == gmax, lane_f, far), axis=-1, keepdims=True)
    w_grp = 1.0 / jnp.sum(jnp.exp(gl - gmax), axis=-1, keepdims=True)
    first = N_GROUPS + EXPERTS_PER_GROUP * grp
    el = jnp.where(jnp.logical_and(lane_f >= first, lane_f < first + EXPERTS_PER_GROUP), logits, -jnp.inf)
    v1 = jnp.max(el, axis=-1, keepdims=True)
    i1 = jnp.min(jnp.where(el == v1, lane_f, far), axis=-1, keepdims=True)
    el2 = jnp.where(lane_f == i1, -jnp.inf, el)
    v2 = jnp.max(el2, axis=-1, keepdims=True)
    i2 = jnp.min(jnp.where(el2 == v2, lane_f, far), axis=-1, keepdims=True)
    t = jnp.exp(v2 - v1)
    w1 = w_grp / (1.0 + t)
    w2 = w_grp * t / (1.0 + t)
    ids = jnp.where(lane == 0, i1 - N_GROUPS, jnp.where(lane == 1, i2 - N_GROUPS, 0.0)).astype(jnp.int32)
    wts = jnp.where(lane == 0, w1, jnp.where(lane == 1, w2, 0.0))
    return ids, wts


def _merge_kernel(x_ref, mod_ref, gm_ref, gf_ref, b0_ref, b1_ref, b2_ref, b3_ref, wg_ref, wb_ref, wo_ref,
                  wrh_ref, wrl_ref, br_ref, x1_ref, h2_ref, ids_ref, wts_ref):
    d = x_ref.shape[-1]
    x = x_ref[0]
    mod = lambda k: mod_ref[0, 0, k:k + 1, :]
    h = _modulated_norm(x, gm_ref[...], mod(0), mod(1)).astype(BF16)
    merged = None
    for i, b_ref in enumerate((b0_ref, b1_ref, b2_ref, b3_ref)):
        gate = jax.nn.sigmoid(jnp.dot(h, wg_ref[:, i * d:(i + 1) * d], preferred_element_type=F32))
        term = gate * jnp.dot(b_ref[0], wb_ref[i], preferred_element_type=F32)
        merged = term if merged is None else merged + term
    o = jnp.dot(merged.astype(BF16), wo_ref[...], preferred_element_type=F32)
    x1 = x + mod(2) * o
    x1_ref[0] = x1
    h2 = _modulated_norm(x1, gf_ref[...], mod(3), mod(4))
    h2_ref[0] = _pack_halves(h2)
    logits = _dot_split(h2, wrh_ref[...], wrl_ref[...]) + br_ref[...]
    ids, wts = _route(logits)
    ids_ref[0] = ids
    wts_ref[0] = wts


def _merge(xc, modv, g_mix, g_ffn, branches, w_gate, w_branch, w_out, w_router, b_router, n_lat):
    bsz, n_all, d = xc.shape
    tm = TOKEN_TILE
    lat_tiles = n_lat // tm
    tile = lambda width: pl.BlockSpec((1, tm, width), lambda b, t: (b, t, 0))
    full = lambda shape: pl.BlockSpec(shape, lambda b, t: (0,) * len(shape))
    return pl.pallas_call(
        _merge_kernel,
        grid=(bsz, n_all // tm),
        in_specs=[tile(d), pl.BlockSpec((1, 1, 6, d), lambda b, t: (b, jnp.minimum(t // lat_tiles, 1), 0, 0)),
                  full((1, d)), full((1, d)),
                  tile(BRANCH_W), tile(BRANCH_W), tile(BRANCH_W), tile(BRANCH_W),
                  full(w_gate.shape), full(w_branch.shape), full(w_out.shape),
                  full(w_router.shape), full(w_router.shape), full(b_router.shape)],
        out_specs=[tile(d), tile(d // 2), tile(LANES), tile(LANES)],
        out_shape=[jax.ShapeDtypeStruct((bsz, n_all, d), F32), jax.ShapeDtypeStruct((bsz, n_all, d // 2), jnp.int32),
                   jax.ShapeDtypeStruct((bsz, n_all, LANES), jnp.int32),
                   jax.ShapeDtypeStruct((bsz, n_all, LANES), F32)],
        compiler_params=_params("parallel", "parallel"),
        name="merge_router",
    )(xc, modv, g_mix, g_ffn, *branches, w_gate, w_branch, w_out, *_split_bf16(w_router), b_router)


def _slot_plan(ids):
    te = EXPERT_TILE
    flat = ids.reshape(-1)
    n_pairs = flat.shape[0]
    n_tiles = n_pairs // te + N_EXPERTS
    onehot = (flat[:, None] == jnp.arange(N_EXPERTS, dtype=jnp.int32)[None, :]).astype(jnp.int32)
    running = jnp.cumsum(onehot, axis=0)
    counts = running[-1]
    rank = jnp.sum(running * onehot, axis=1) - 1
    padded = ((counts + te - 1) // te) * te
    ends = jnp.cumsum(padded)
    offsets = ends - padded
    dest = (jnp.sum(onehot * offsets[None, :], axis=1) + rank).astype(jnp.int32)
    tile_start = jnp.arange(n_tiles, dtype=jnp.int32) * te
    tile_expert = jnp.minimum(jnp.sum((tile_start[:, None] >= ends[None, :]).astype(jnp.int32), axis=1),
                              N_EXPERTS - 1).astype(jnp.int32)
    n_used = (ends[-1] // te).astype(jnp.int32).reshape(1)
    return dest, tile_expert, n_used, n_tiles


def _row_copy(src_ref, src_row, dst_ref, dst_row, sem):
    return pltpu.make_async_copy(src_ref.at[pl.ds(src_row, 1), :], dst_ref.at[pl.ds(dst_row, 1), :], sem)


ROW_DMA_UNROLL = 8


def _wait_rows(src_ref, dst_ref, sem, n_rows):
    def wait(r, carry):
        for _ in range(ROW_DMA_UNROLL):
            _row_copy(src_ref, 0, dst_ref, 0, sem).wait()
        return carry

    lax.fori_loop(0, n_rows // ROW_DMA_UNROLL, wait, 0)


def _dispatch_kernel(dest_ref, h_ref, init_ref, out_ref, stage, stage_sems, row_sem, *, tm):
    del init_ref
    i = pl.program_id(0)
    n_steps = pl.num_programs(0)
    slot = i % 2

    def stage_copy(tile, into):
        return pltpu.make_async_copy(h_ref.at[pl.ds(tile * tm, tm), :], stage.at[into], stage_sems.at[into])

    @pl.when(i == 0)
    def _():
        stage_copy(0, 0).start()

    @pl.when(i > 0)
    def _():
        _wait_rows(stage.at[0], out_ref, row_sem, 2 * tm)

    @pl.when(i + 1 < n_steps)
    def _():
        stage_copy(i + 1, 1 - slot).start()

    stage_copy(i, slot).wait()

    def start(g, carry):
        for u in range(ROW_DMA_UNROLL):
            r = g * ROW_DMA_UNROLL + u
            for k in range(2):
                _row_copy(stage.at[slot], r, out_ref, dest_ref[0, 0, 2 * r + k], row_sem).start()
        return carry

    lax.fori_loop(0, tm // ROW_DMA_UNROLL, start, 0)

    @pl.when(i == n_steps - 1)
    def _():
        _wait_rows(stage.at[0], out_ref, row_sem, 2 * tm)


def _dispatch(dest, h2, n_slots):
    n_tok, d = h2.shape
    tm = TOKEN_TILE
    return pl.pallas_call(
        functools.partial(_dispatch_kernel, tm=tm),
        grid=(n_tok // tm,),
        in_specs=[pl.BlockSpec((1, 1, 2 * tm), lambda i: (i, 0, 0), memory_space=pltpu.SMEM),
                  pl.BlockSpec(memory_space=pl.ANY),
                  pl.BlockSpec(memory_space=pl.ANY)],
        out_specs=pl.BlockSpec(memory_space=pl.ANY),
        out_shape=jax.ShapeDtypeStruct((n_slots, d), h2.dtype),
        scratch_shapes=[pltpu.VMEM((2, tm, d), h2.dtype), pltpu.SemaphoreType.DMA((2,)),
                        pltpu.SemaphoreType.DMA(())],
        input_output_aliases={2: 0},
        compiler_params=_params("arbitrary"),
        name="moe_dispatch",
    )(dest.reshape(n_tok // tm, 1, 2 * tm), h2, jnp.zeros((n_slots, d), h2.dtype))


def _ffn_kernel(te_ref, used_ref, x_ref, wg_ref, wu_ref, wd_ref, o_ref):
    del te_ref
    i = pl.program_id(0)

    @pl.when(i < used_ref[0])
    def _():
        hi, lo = _unpack_halves(x_ref[...])
        x = jnp.concatenate([hi.astype(BF16), lo.astype(BF16)], axis=1)
        g = jnp.dot(x, wg_ref[0], preferred_element_type=F32)
        u = jnp.dot(x, wu_ref[0], preferred_element_type=F32)
        a = (g * jax.nn.sigmoid(g) * u).astype(BF16)
        o_ref[...] = _pack_halves(jnp.dot(a, wd_ref[0], preferred_element_type=F32))

    @pl.when(i >= used_ref[0])
    def _():
        o_ref[...] = jnp.zeros_like(o_ref)


def _expert_ffn(tile_expert, n_used, xs, w_gate, w_up, w_down, n_tiles):
    n_slots, half = xs.shape
    d = 2 * half
    te = EXPERT_TILE
    de = w_gate.shape[-1]
    grid_spec = pltpu.PrefetchScalarGridSpec(
        num_scalar_prefetch=2,
        grid=(n_tiles,),
        in_specs=[pl.BlockSpec((te, half), lambda i, te_ref, used: (i, 0)),
                  pl.BlockSpec((1, d, de), lambda i, te_ref, used: (te_ref[i], 0, 0)),
                  pl.BlockSpec((1, d, de), lambda i, te_ref, used: (te_ref[i], 0, 0)),
                  pl.BlockSpec((1, de, d), lambda i, te_ref, used: (te_ref[i], 0, 0))],
        out_specs=pl.BlockSpec((te, half), lambda i, te_ref, used: (i, 0)),
    )
    return pl.pallas_call(
        _ffn_kernel,
        grid_spec=grid_spec,
        out_shape=jax.ShapeDtypeStruct((n_slots, half), jnp.int32),
        compiler_params=_params("arbitrary"),
        name="moe_ffn",
    )(tile_expert, n_used, xs, w_gate, w_up, w_down)


def _combine_kernel(dest_ref, next_ref, wts_ref, x1_ref, mod_ref, gf_ref, ys_ref, o_ref, buf, sems, *, final):
    tm = x1_ref.shape[1]
    step = pl.program_id(0) * pl.num_programs(1) + pl.program_id(1)
    n_steps = pl.num_programs(0) * pl.num_programs(1)
    slot = step % 2

    def gather(idx_ref, into):
        def start(g, carry):
            for u in range(ROW_DMA_UNROLL):
                r = g * ROW_DMA_UNROLL + u
                for k in range(2):
                    _row_copy(ys_ref, idx_ref[0, 0, 2 * r + k], buf.at[into, k], r, sems.at[into]).start()
            return carry

        lax.fori_loop(0, tm // ROW_DMA_UNROLL, start, 0)

    @pl.when(step == 0)
    def _():
        gather(dest_ref, 0)

    @pl.when(step + 1 < n_steps)
    def _():
        gather(next_ref, 1 - slot)

    _wait_rows(ys_ref, buf.at[slot, 0], sems.at[slot], 2 * tm)
    wts = wts_ref[0]
    parts = []
    for k in range(2):
        hi, lo = _unpack_halves(buf[slot, k])
        parts.append(wts[:, k:k + 1] * jnp.concatenate([hi, lo], axis=1))
    x2 = x1_ref[0] + mod_ref[0, 0, 5:6, :] * (parts[0] + parts[1])
    if final:
        ms = jnp.mean(x2 * x2, axis=-1, keepdims=True)
        x2 = x2 * lax.rsqrt(ms + RMS_EPS) * gf_ref[...]
    o_ref[0] = x2


def _combine(dest, wts, x1, modv, g_final, ys, n_lat, final):
    bsz, n_all, d = x1.shape
    tm = TOKEN_TILE
    lat_tiles = n_lat // tm
    all_tiles = n_all // tm
    tiles = lat_tiles if final else all_tiles
    n_out = n_lat if final else n_all

    def dest_row(step):
        step = jnp.minimum(step, bsz * tiles - 1)
        return (step // tiles) * all_tiles + step % tiles

    idx_spec = lambda ahead: pl.BlockSpec((1, 1, 2 * tm), lambda b, t: (dest_row(b * tiles + t + ahead), 0, 0),
                                          memory_space=pltpu.SMEM)
    dest3 = dest.reshape(bsz * all_tiles, 1, 2 * tm)
    return pl.pallas_call(
        functools.partial(_combine_kernel, final=final),
        grid=(bsz, tiles),
        in_specs=[idx_spec(0), idx_spec(1),
                  pl.BlockSpec((1, tm, LANES), lambda b, t: (b, t, 0)),
                  pl.BlockSpec((1, tm, d), lambda b, t: (b, t, 0)),
                  pl.BlockSpec((1, 1, 6, d), lambda b, t: (b, jnp.minimum(t // lat_tiles, 1), 0, 0)),
                  pl.BlockSpec((1, d), lambda b, t: (0, 0)),
                  pl.BlockSpec(memory_space=pl.ANY)],
        out_specs=pl.BlockSpec((1, tm, d), lambda b, t: (b, t, 0)),
        out_shape=jax.ShapeDtypeStruct((bsz, n_out, d), F32),
        scratch_shapes=[pltpu.VMEM((2, 2, tm, d // 2), jnp.int32), pltpu.SemaphoreType.DMA((2,))],
        compiler_params=_params("arbitrary", "arbitrary"),
        name="moe_combine",
    )(dest3, dest3, wts, x1, modv, g_final, ys)


def kernel(x, c, ctx, c_ctx, w_mod, b_mod, g_mix, g_ffn, w_in, pool_w, pool_scale, diff_lambda, diff_norm_g, nat_rpb, gqa_q_norm, gqa_k_norm, w_branch, w_out, w_router_group, b_router_group, w_router_expert, b_router_expert, w_exp_gate, w_exp_up, w_exp_down, g_final):
    bsz, n_lat, d = x.shape
    n_ctx = ctx.shape[1]
    n_all = n_lat + n_ctx
    depth = w_mod.shape[0]
    tm = TOKEN_TILE
    assert n_lat % tm == 0 and n_ctx % tm == 0 and n_lat % GRID_W == 0 and n_lat // GRID_W >= NAT_WIN_R
    assert d == w_in.shape[1] and w_in.shape[2] == PROJ_COLS + N_BRANCH * d

    xc = jnp.concatenate([x, ctx], axis=1)
    mod_rows = 16
    assert bsz + 1 <= mod_rows
    cc = jnp.zeros((mod_rows, d), F32).at[:bsz].set(c).at[bsz].set(c_ctx)
    mod = _modulation(cc, w_mod, b_mod)
    mod_lat = mod[:, :bsz].reshape(depth, bsz, 1, 6, d)
    mod_ctx = jnp.broadcast_to(mod[:, bsz].reshape(depth, 1, 1, 6, d), (depth, bsz, 1, 6, d))
    modv = jnp.concatenate([mod_lat, mod_ctx], axis=2)

    tabs = _rope_tables(n_lat, n_ctx, 256, DIFF_DQ) + _rope_tables(n_lat, n_ctx, 256, HEAD_DIM)
    head_of = np.arange(LANES) // HEAD_DIM
    ones_bd = jnp.asarray((head_of[:, None] == head_of[None, :]).astype(np.float32) / HEAD_DIM, BF16)
    group_of = np.arange(BRANCH_W) // POOL_GW
    pool_mask = jnp.asarray(group_of[:, None] == group_of[None, :])

    out = None
    for l in range(depth):
        final = l == depth - 1
        w_proj = w_in[l, :, :PROJ_COLS].astype(BF16)
        w_gate = w_in[l, :, PROJ_COLS:].astype(BF16)
        tile2 = lambda v: jnp.tile(v.reshape(1, HEAD_DIM), (1, LANES // HEAD_DIM))
        u, proj = _inproj(xc, modv[l], g_mix[l].reshape(1, d), w_proj, tabs, ones_bd,
                          tile2(gqa_q_norm[l]), tile2(gqa_k_norm[l]), n_lat)
        pool_bd = jnp.where(pool_mask, jnp.tile(pool_w[l].reshape(BRANCH_W, POOL_GW), (1, BRANCH_W // POOL_GW)),
                            0.0).astype(BF16)
        b_pool = _pool(u, pool_bd, pool_scale[l].reshape(1, BRANCH_W), n_lat)
        b_diff = _diff_attention(proj, diff_lambda[l], tile2(diff_norm_g[l]), ones_bd, n_lat, l)
        b_nat = _nat_attention(proj, _nat_bias(nat_rpb[l]), n_lat)
        b_gqa = _gqa_attention(proj, n_lat)
        n_route = N_GROUPS + N_EXPERTS
        w_router = jnp.zeros((d, LANES), F32).at[:, :N_GROUPS].set(w_router_group[l])
        w_router = w_router.at[:, N_GROUPS:n_route].set(w_router_expert[l])
        b_router = jnp.zeros((1, LANES), F32).at[0, :N_GROUPS].set(b_router_group[l])
        b_router = b_router.at[0, N_GROUPS:n_route].set(b_router_expert[l])
        x1, h2, ids, wts = _merge(xc, modv[l], g_mix[l].reshape(1, d), g_ffn[l].reshape(1, d),
                                  (b_pool, b_diff, b_nat, b_gqa), w_gate, w_branch[l].astype(BF16),
                                  w_out[l].astype(BF16), w_router, b_router, n_lat)
        dest, tile_expert, n_used, n_tiles = _slot_plan(ids[:, :, :2])
        xs = _dispatch(dest, h2.reshape(bsz * n_all, d // 2), n_tiles * EXPERT_TILE)
        ys = _expert_ffn(tile_expert, n_used, xs, w_exp_gate[l].astype(BF16), w_exp_up[l].astype(BF16),
                         w_exp_down[l].astype(BF16), n_tiles)
        out = _combine(dest, wts, x1, modv[l], g_final.reshape(1, d), ys, n_lat, final)
        xc = out
    return out
```

```python
import functools
import math

import numpy as np
import jax
import jax.numpy as jnp
from jax import lax
from jax.experimental import pallas as pl
from jax.experimental.pallas import tpu as pltpu

GRID_W = 64
HEAD_DIM = 64
BRANCH_W = 256
N_BRANCH = 4
POOL_HALF_WINDOWS = (1, 2, 4, 8)
POOL_GW = 64
DIFF_DQ = 32
NAT_WIN_R = 8
NAT_WIN_C = 16
N_GROUPS = 4
EXPERTS_PER_GROUP = 8
N_EXPERTS = 32
D_EXPERT = 512
ROPE_THETA = 10000.0
RMS_EPS = 1e-6
NEG_INF = -1e30
LOG2E = 1.4426950408889634

LANES = 128
TOKEN_TILE = 256
EXPERT_TILE = 256
VMEM_LIMIT_BYTES = 56 * 1024 * 1024

PROJ_COLS = 2304
COL_DQ, COL_DK, COL_DV = 0, 256, 512
COL_NQ, COL_NK, COL_NV = 768, 1024, 1280
COL_GQ, COL_GK, COL_GV = 1536, 2048, 2176

F32 = jnp.float32
BF16 = jnp.bfloat16
HIGHEST = lax.Precision.HIGHEST


def _params(*sem):
    return pltpu.CompilerParams(dimension_semantics=sem, vmem_limit_bytes=VMEM_LIMIT_BYTES)


def _lane_iota(shape):
    return lax.broadcasted_iota(jnp.int32, shape, len(shape) - 1)


def _dot_nt(a, b):
    return lax.dot_general(a, b, (((1,), (1,)), ((), ())), preferred_element_type=F32)


def _split_bf16(x):
    hi = x.astype(BF16)
    return hi, (x - hi.astype(F32)).astype(BF16)


def _dot_split(a, b_hi, b_lo=None):
    a_hi, a_lo = _split_bf16(a)
    out = jnp.dot(a_hi, b_hi, preferred_element_type=F32) + jnp.dot(a_lo, b_hi, preferred_element_type=F32)
    if b_lo is not None:
        out = out + jnp.dot(a_hi, b_lo, preferred_element_type=F32)
    return out


def _pack_halves(y):
    w = y.shape[1] // 2
    hi = pltpu.bitcast(y[:, :w].astype(BF16).astype(F32), jnp.int32)
    lo = pltpu.bitcast(y[:, w:].astype(BF16).astype(F32), jnp.int32)
    return jnp.bitwise_or(hi, lax.shift_right_logical(lo, 16))


def _unpack_halves(word):
    hi = pltpu.bitcast(jnp.bitwise_and(word, -65536), F32)
    lo = pltpu.bitcast(lax.shift_left(word, 16), F32)
    return hi, lo


def _modulated_norm(x, gain, shift, scale):
    ms = jnp.mean(x * x, axis=-1, keepdims=True)
    return (x * lax.rsqrt(ms + RMS_EPS)) * gain * (1.0 + scale) + shift


def _mod_kernel(c_ref, w_ref, b_ref, o_ref):
    c = c_ref[...]
    s = c * jax.nn.sigmoid(c)
    o_ref[0] = jnp.dot(s, w_ref[0], preferred_element_type=F32, precision=HIGHEST) + b_ref[0]


def _modulation(cc, w_mod, b_mod):
    depth, d, n6 = w_mod.shape
    rows = cc.shape[0]
    tn = 512
    return pl.pallas_call(
        _mod_kernel,
        grid=(depth, n6 // tn),
        in_specs=[pl.BlockSpec((rows, d), lambda l, j: (0, 0)),
                  pl.BlockSpec((1, d, tn), lambda l, j: (l, 0, j)),
                  pl.BlockSpec((1, 1, tn), lambda l, j: (l, 0, j))],
        out_specs=pl.BlockSpec((1, rows, tn), lambda l, j: (l, 0, j)),
        out_shape=jax.ShapeDtypeStruct((depth, rows, n6), F32),
        compiler_params=_params("arbitrary", "arbitrary"),
        name="modulation",
    )(cc, w_mod, b_mod.reshape(depth, 1, n6))


def _rope_tables(n_lat, n_ctx, width, vec_dim):
    t = np.arange(n_lat)
    row, col = t // GRID_W, t % GRID_W
    j = np.arange(width) % vec_dim
    half = vec_dim // 2
    quarter = half // 2
    jj = j % half
    inv = ROPE_THETA ** (-(jj % quarter).astype(np.float64) / quarter)
    pos = np.where((j < half)[None, :], row[:, None], col[:, None]).astype(np.float64)
    ang = pos * inv[None, :]
    sign = np.where(jj < quarter, -1.0, 1.0)[None, :]
    cos = np.concatenate([np.cos(ang), np.ones((n_ctx, width))], axis=0)
    sin = np.concatenate([np.sin(ang) * sign, np.zeros((n_ctx, width))], axis=0)
    return jnp.asarray(cos, F32), jnp.asarray(sin, F32)


def _rope(x, cos, sin, quarter):
    lane = _lane_iota(x.shape)
    first = (lane % (2 * quarter)) < quarter
    up = pltpu.roll(x, LANES - quarter, 1)
    down = pltpu.roll(x, quarter, 1)
    return x * cos + jnp.where(first, up, down) * sin


def _head_rms(x, ones_bd, gain):
    ms = _dot_split(x * x, ones_bd)
    return x * lax.rsqrt(ms + RMS_EPS) * gain


def _token_specs(tokens, tm, lat_tiles):
    lat, ctx, ctx_first = tokens
    d = lat.shape[-1]
    return [pl.BlockSpec((1, tm, d), lambda b, t: (b, jnp.minimum(t, lat_tiles - 1), 0)),
            pl.BlockSpec((1, tm, d), lambda b, t: (b, ctx_first + jnp.maximum(t - lat_tiles, 0), 0))]


def _token_tile(xa_ref, xb_ref, lat_tiles):
    return jnp.where(pl.program_id(1) >= lat_tiles, xb_ref[0], xa_ref[0])


def _inproj_kernel(xa_ref, xb_ref, mod_ref, g_ref, w_ref, cd_ref, sd_ref, cg_ref, sg_ref, ones_ref, qn_ref, kn_ref,
                   u_ref, p_ref, *, lat_tiles):
    x = _token_tile(xa_ref, xb_ref, lat_tiles)
    h = _modulated_norm(x, g_ref[...], mod_ref[0, 0, 0:1, :], mod_ref[0, 0, 1:2, :]).astype(BF16)

    def proj(col):
        y = jnp.dot(h, w_ref[:, col:col + 2 * LANES], preferred_element_type=F32)
        return y[:, 0:LANES], y[:, LANES:2 * LANES]

    for b, y in enumerate(proj(0)):
        u_ref[0, :, b * LANES:(b + 1) * LANES] = y
    for dst, src, mult in ((COL_DQ, 256, DIFF_DQ ** -0.5 * LOG2E), (COL_DK, 512, 1.0)):
        for b, y in enumerate(proj(src)):
            sl = slice(b * LANES, (b + 1) * LANES)
            y = _rope(y, cd_ref[:, sl], sd_ref[:, sl], DIFF_DQ // 4) * mult
            p_ref[0, :, dst + b * LANES:dst + (b + 1) * LANES] = y.astype(BF16)
    for dst, src in ((COL_DV, 768), (COL_NQ, 1024), (COL_NK, 1280), (COL_NV, 1536)):
        for b, y in enumerate(proj(src)):
            p_ref[0, :, dst + b * LANES:dst + (b + 1) * LANES] = y.astype(BF16)
    ones_bd = ones_ref[...]
    lane = _lane_iota((x.shape[0], LANES))
    low = lane < HEAD_DIM
    for b, y in enumerate(proj(1792)):
        sl = slice(b * LANES, (b + 1) * LANES)
        y = _head_rms(y, ones_bd, qn_ref[...])
        y = _rope(y, cg_ref[:, sl], sg_ref[:, sl], HEAD_DIM // 4) * (HEAD_DIM ** -0.5 * LOG2E)
        swapped = pltpu.roll(y, HEAD_DIM, 1)
        keep = low if b == 0 else jnp.logical_not(low)
        even = jnp.where(keep, y if b == 0 else swapped, 0.0)
        odd = jnp.where(keep, swapped if b == 0 else y, 0.0)
        p_ref[0, :, COL_GQ + (2 * b) * LANES:COL_GQ + (2 * b + 1) * LANES] = even.astype(BF16)
        p_ref[0, :, COL_GQ + (2 * b + 1) * LANES:COL_GQ + (2 * b + 2) * LANES] = odd.astype(BF16)
    gk, gv = proj(2048)
    y = _head_rms(gk, ones_bd, kn_ref[...])
    y = _rope(y, cg_ref[:, 0:LANES], sg_ref[:, 0:LANES], HEAD_DIM // 4)
    p_ref[0, :, COL_GK:COL_GK + LANES] = y.astype(BF16)
    p_ref[0, :, COL_GV:COL_GV + LANES] = gv.astype(BF16)


def _inproj(tokens, n_all, modv, g_mix, w_proj, tabs, ones_bd, q_norm, k_norm, n_lat):
    bsz, _, d = tokens[0].shape
    tm = TOKEN_TILE
    lat_tiles = n_lat // tm
    cd, sd, cg, sg = tabs
    full = lambda shape: pl.BlockSpec(shape, lambda b, t: (0,) * len(shape))
    tab = pl.BlockSpec((tm, 256), lambda b, t: (t, 0))
    return pl.pallas_call(
        functools.partial(_inproj_kernel, lat_tiles=lat_tiles),
        grid=(bsz, n_all // tm),
        in_specs=[*_token_specs(tokens, tm, lat_tiles),
                  pl.BlockSpec((1, 1, 6, d), lambda b, t: (b, jnp.minimum(t // lat_tiles, 1), 0, 0)),
                  full((1, d)), full((d, PROJ_COLS)), tab, tab, tab, tab,
                  full((LANES, LANES)), full((1, LANES)), full((1, LANES))],
        out_specs=[pl.BlockSpec((1, tm, BRANCH_W), lambda b, t: (b, t, 0)),
                   pl.BlockSpec((1, tm, PROJ_COLS), lambda b, t: (b, t, 0))],
        out_shape=[jax.ShapeDtypeStruct((bsz, n_all, BRANCH_W), F32),
                   jax.ShapeDtypeStruct((bsz, n_all, PROJ_COLS), BF16)],
        compiler_params=_params("parallel", "parallel"),
        name="inproj",
    )(tokens[0], tokens[1], modv, g_mix, w_proj, cd, sd, cg, sg, ones_bd, q_norm, k_norm)


def _pool_kernel(prev_ref, u_ref, next_ref, w_ref, scale_ref, o_ref, pad_ref, *, n_lat, n_all):
    t = pl.program_id(1)
    tm = TOKEN_TILE
    lat_tiles = n_lat // tm
    all_tiles = n_all // tm
    halo = 8
    u = u_ref[0]
    first = jnp.logical_or(t == 0, t == lat_tiles)
    last = jnp.logical_or(t == lat_tiles - 1, t == all_tiles - 1)
    pad_ref[0:halo, :] = jnp.where(first, 0.0, prev_ref[0])
    pad_ref[halo:halo + tm, :] = u
    pad_ref[halo + tm:2 * halo + tm, :] = jnp.where(last, 0.0, next_ref[0])

    def shifted(d):
        return pad_ref[halo + d:halo + d + tm, :]

    sums = []
    acc = None
    for hw in POOL_HALF_WINDOWS:
        lo = hw // 2 if acc is not None else 0
        for d in range(lo, hw):
            term = shifted(d) + shifted(-d - 1)
            acc = term if acc is None else acc + term
        sums.append(acc)
    lane = _lane_iota((tm, BRANCH_W))
    group = lane // POOL_GW
    win = jnp.where(group == 0, sums[0], jnp.where(group == 1, sums[1], jnp.where(group == 2, sums[2], sums[3])))
    half = jnp.where(group == 0, 1, jnp.where(group == 1, 2, jnp.where(group == 2, 4, 8)))
    seg_start = jnp.where(t >= lat_tiles, n_lat, 0)
    seg_len = jnp.where(t >= lat_tiles, n_all - n_lat, n_lat)
    pos = t * tm - seg_start + lax.broadcasted_iota(jnp.int32, (tm, BRANCH_W), 0)
    cnt = jnp.minimum(pos + half, seg_len) - jnp.maximum(pos - half, 0)
    pooled = (win / cnt.astype(F32) - u).astype(BF16)
    y = jnp.dot(pooled, w_ref[...], preferred_element_type=F32) * scale_ref[...]
    o_ref[0] = y.astype(BF16)


def _pool(u, w_bd, pool_scale, n_lat, n_out):
    bsz, n_all, ch = u.shape
    tm = TOKEN_TILE
    halo = 8
    per = tm // halo
    nblk = n_all // halo
    return pl.pallas_call(
        functools.partial(_pool_kernel, n_lat=n_lat, n_all=n_all),
        grid=(bsz, n_out // tm),
        in_specs=[pl.BlockSpec((1, halo, ch), lambda b, t: (b, jnp.maximum(t * per - 1, 0), 0)),
                  pl.BlockSpec((1, tm, ch), lambda b, t: (b, t, 0)),
                  pl.BlockSpec((1, halo, ch), lambda b, t: (b, jnp.minimum((t + 1) * per, nblk - 1), 0)),
                  pl.BlockSpec((ch, ch), lambda b, t: (0, 0)),
                  pl.BlockSpec((1, ch), lambda b, t: (0, 0))],
        out_specs=pl.BlockSpec((1, tm, ch), lambda b, t: (b, t, 0)),
        out_shape=jax.ShapeDtypeStruct((bsz, n_out, ch), BF16),
        scratch_shapes=[pltpu.VMEM((tm + 2 * halo, ch), F32)],
        compiler_params=_params("parallel", "parallel"),
        name="pool",
    )(u, u, u, w_bd, pool_scale)


ATTN_Q_TILE = 256
ATTN_ROW_BLOCK = 64
GQA_K_TILE = 1024
DIFF_K_TILE = 2048


def _flash_scratch(rows, n_ctx, tk):
    wide = lambda: pltpu.VMEM((rows, LANES), F32)
    return [pltpu.VMEM((rows, LANES), BF16), pltpu.VMEM((2, rows, tk), F32),
            pltpu.VMEM((rows, n_ctx), F32), pltpu.VMEM((rows, tk), BF16), wide(), wide(), wide(), wide()]


def _flash_attention(k_ref, v_ref, scratch, is_latent_block, n_lat, n_all, replicated_stats):
    q_scr, s_scr, sc_scr, p_scr, m_scr, l_scr, a_scr, acc_scr = scratch
    rows = q_scr.shape[0]
    tk = p_scr.shape[1]
    rb = ATTN_ROW_BLOCK
    n_chunks = n_lat // tk
    n_ctx = n_all - n_lat
    assert n_chunks % 2 == 0 and n_chunks * tk == n_lat and n_ctx <= tk

    m_scr[...] = jnp.full(m_scr.shape, -jnp.inf, F32)
    l_scr[...] = jnp.zeros(l_scr.shape, F32)
    acc_scr[...] = jnp.zeros(acc_scr.shape, F32)

    def produce(dst, start, size):
        dst[...] = _dot_nt(q_scr[...], k_ref[0, pl.ds(start, size), :])

    def consume(src, start, size):
        for r in range(rows // rb):
            rs = slice(r * rb, (r + 1) * rb)
            s = src[rs, :]
            if replicated_stats:
                m_old = m_scr[rs, :]
                m_new = jnp.maximum(m_old, jnp.broadcast_to(jnp.max(s, axis=-1, keepdims=True), (rb, LANES)))
                m_wide, m_cols = m_new, [m_new] * (size // LANES)
            else:
                m_old = m_scr[rs, 0:1]
                m_new = jnp.maximum(m_old, jnp.max(s, axis=-1, keepdims=True))
                m_wide, m_cols = jnp.broadcast_to(m_new, (rb, LANES)), [m_new] * (size // LANES)
            alpha = jnp.exp2(m_old - m_new)
            part = None
            for b in range(size // LANES):
                cols = slice(b * LANES, (b + 1) * LANES)
                p = jnp.exp2(s[:, cols] - m_cols[b])
                p_scr[rs, cols] = p.astype(BF16)
                part = p if part is None else part + p
            l_scr[rs, :] = alpha * l_scr[rs, :] + part
            m_scr[rs, :] = m_wide
            a_scr[rs, :] = jnp.broadcast_to(alpha, (rb, LANES))
        pv = jnp.dot(p_scr[:, 0:size], v_ref[0, pl.ds(start, size), :], preferred_element_type=F32)
        acc_scr[...] = a_scr[...] * acc_scr[...] + pv

    def lat(chunk):
        return pl.multiple_of(chunk * tk, tk)

    produce(sc_scr, n_lat, n_ctx)

    @pl.when(is_latent_block)
    def _():
        produce(s_scr.at[0], 0, tk)

        def step(i, carry):
            produce(s_scr.at[1], lat(2 * i + 1), tk)
            consume(s_scr.at[0], lat(2 * i), tk)
            produce(s_scr.at[0], lat(2 * i + 2), tk)
            consume(s_scr.at[1], lat(2 * i + 1), tk)
            return carry

        lax.fori_loop(0, n_chunks // 2 - 1, step, 0)
        produce(s_scr.at[1], (n_chunks - 1) * tk, tk)
        consume(s_scr.at[0], (n_chunks - 2) * tk, tk)
        consume(s_scr.at[1], (n_chunks - 1) * tk, tk)

    consume(sc_scr, n_lat, n_ctx)
    return acc_scr[...] / jnp.sum(l_scr[...], axis=-1, keepdims=True)


def _diff_kernel(lam_ref, g_ref, ones_ref, q_ref, k_ref, v_ref, o_ref, *scratch, n_lat, n_all, tq, lam_init):
    i = pl.program_id(2)
    lf = lam_ref[...]
    lam = (jnp.exp(jnp.sum(lf[0:1] * lf[1:2], axis=-1, keepdims=True))
           - jnp.exp(jnp.sum(lf[2:3] * lf[3:4], axis=-1, keepdims=True)) + lam_init)
    q = q_ref[0]
    lane = _lane_iota((tq, LANES))
    q_scr = scratch[0]
    for n in range(LANES // DIFF_DQ):
        start = n * DIFF_DQ
        mine = jnp.logical_and(lane >= start, lane < start + DIFF_DQ)
        q_scr[n * tq:(n + 1) * tq, :] = jnp.where(mine, q, jnp.zeros_like(q))
    out = _flash_attention(k_ref, v_ref, scratch, i * tq < n_lat, n_lat, n_all, replicated_stats=False)
    o = jnp.where(lane < HEAD_DIM, out[0:tq] - lam * out[tq:2 * tq], out[2 * tq:3 * tq] - lam * out[3 * tq:4 * tq])
    ms = _dot_split(o * o, ones_ref[...])
    o_ref[0] = (o * lax.rsqrt(ms + RMS_EPS) * g_ref[...] * (1.0 - lam_init)).astype(BF16)


def _diff_attention(proj, lam_p, norm_g, ones_bd, n_lat, layer, n_out):
    bsz, n_all, _ = proj.shape
    tq = ATTN_Q_TILE
    lam_init = 0.8 - 0.6 * math.exp(-0.3 * layer)
    qb, kb, vb = COL_DQ // LANES, COL_DK // LANES, COL_DV // LANES
    return pl.pallas_call(
        functools.partial(_diff_kernel, n_lat=n_lat, n_all=n_all, tq=tq, lam_init=lam_init),
        grid=(bsz, 2, n_out // tq),
        in_specs=[pl.BlockSpec(lam_p.shape, lambda b, p, i: (0, 0)),
                  pl.BlockSpec((1, LANES), lambda b, p, i: (0, 0)),
                  pl.BlockSpec((LANES, LANES), lambda b, p, i: (0, 0)),
                  pl.BlockSpec((1, tq, LANES), lambda b, p, i: (b, i, qb + p)),
                  pl.BlockSpec((1, n_all, LANES), lambda b, p, i: (b, 0, kb + p)),
                  pl.BlockSpec((1, n_all, LANES), lambda b, p, i: (b, 0, vb + p))],
        out_specs=pl.BlockSpec((1, tq, LANES), lambda b, p, i: (b, i, p)),
        out_shape=jax.ShapeDtypeStruct((bsz, n_out, BRANCH_W), BF16),
        scratch_shapes=_flash_scratch(4 * tq, n_all - n_lat, min(DIFF_K_TILE, n_lat // 2)),
        compiler_params=_params("parallel", "parallel", "parallel"),
        name="diff_attention",
    )(lam_p, norm_g, ones_bd, proj, proj, proj)


def _gqa_kernel(q_ref, k_ref, v_ref, o_ref, *scratch, n_lat, n_all, tq):
    kv = pl.program_id(1)
    i = pl.program_id(2)
    q_scr = scratch[0]
    for g in range(2):
        q_scr[g * tq:(g + 1) * tq, :] = q_ref[0, :, g * LANES:(g + 1) * LANES]
    out = _flash_attention(k_ref, v_ref, scratch, i * tq < n_lat, n_lat, n_all, replicated_stats=True)
    outs = [out[0:tq], out[tq:2 * tq]]
    placed = [jnp.where(kv == g, outs[g], pltpu.roll(outs[g], HEAD_DIM, 1)) for g in range(2)]
    lane = _lane_iota((tq, LANES))
    o_ref[0] = jnp.where(lane < HEAD_DIM, placed[0], placed[1]).astype(BF16)


def _gqa_attention(proj, n_lat, n_out):
    bsz, n_all, _ = proj.shape
    tq = ATTN_Q_TILE
    qb, kb, vb = COL_GQ // (2 * LANES), COL_GK // LANES, COL_GV // LANES
    return pl.pallas_call(
        functools.partial(_gqa_kernel, n_lat=n_lat, n_all=n_all, tq=tq),
        grid=(bsz, 2, n_out // tq),
        in_specs=[pl.BlockSpec((1, tq, 2 * LANES), lambda b, p, i: (b, i, qb + p)),
                  pl.BlockSpec((1, n_all, LANES), lambda b, p, i: (b, 0, kb)),
                  pl.BlockSpec((1, n_all, LANES), lambda b, p, i: (b, 0, vb))],
        out_specs=pl.BlockSpec((1, tq, LANES), lambda b, p, i: (b, i, p)),
        out_shape=jax.ShapeDtypeStruct((bsz, n_out, BRANCH_W), BF16),
        scratch_shapes=_flash_scratch(2 * tq, n_all - n_lat, min(GQA_K_TILE, n_lat // 2)),
        compiler_params=_params("parallel", "parallel", "parallel"),
        name="gqa_attention",
    )(proj, proj, proj)


def _nat_bias(rpb):
    kc = NAT_WIN_C
    j = np.arange(GRID_W)
    col_start = np.clip(j - kc // 2, 0, GRID_W - kc)
    valid = (j[None, :] >= col_start[:, None]) & (j[None, :] < col_start[:, None] + kc)
    dc = np.clip(j[None, :] - j[:, None], -(kc - 1), kc - 1) + (kc - 1)
    tab = rpb[:, :, dc]
    tab = jnp.where(jnp.asarray(valid)[None, None], tab.astype(F32), NEG_INF)
    heads = rpb.shape[0]
    slabs = []
    for dr0 in range(NAT_WIN_R):
        blk = tab[:, dr0:dr0 + NAT_WIN_R]
        slabs.append(blk.transpose(0, 2, 1, 3).reshape(heads * GRID_W, NAT_WIN_R * GRID_W))
    return jnp.stack(slabs)


NAT_ROWS_PER_STEP = 4


def _nat_kernel(bias_ref, q_ref, k_ref, v_ref, o_ref, *, n_lat, n_all, n_out):
    w = GRID_W
    rows = n_lat // w
    heads = BRANCH_W // HEAD_DIM
    scale = HEAD_DIM ** -0.5
    lane = _lane_iota((w, BRANCH_W))
    head_of_lane = lane // HEAD_DIM
    k_ctx = k_ref[0, n_lat:n_all, :]
    v_ctx = v_ref[0, n_lat:n_all, :]

    def stacked_queries(start):
        q = q_ref[0, pl.ds(start, w), :]
        return jnp.concatenate([jnp.where(head_of_lane == h, q, jnp.zeros_like(q)) for h in range(heads)], axis=0)

    def unstack(o):
        out = jnp.zeros((w, BRANCH_W), F32)
        for h in range(heads):
            out = jnp.where(head_of_lane == h, o[h * w:(h + 1) * w, :], out)
        return out

    def grid_row(r):
        r0 = jnp.clip(r - NAT_WIN_R // 2, 0, rows - NAT_WIN_R)
        dr0 = r0 - r + (NAT_WIN_R - 1)
        qs = stacked_queries(pl.multiple_of(r * w, w))
        kb = k_ref[0, pl.ds(pl.multiple_of(r0 * w, w), NAT_WIN_R * w), :]
        vb = v_ref[0, pl.ds(pl.multiple_of(r0 * w, w), NAT_WIN_R * w), :]
        s_band = _dot_nt(qs, kb) * scale + bias_ref[dr0]
        s_ctx = _dot_nt(qs, k_ctx) * scale
        m = jnp.maximum(jnp.max(s_band, axis=-1, keepdims=True), jnp.max(s_ctx, axis=-1, keepdims=True))
        p_band = jnp.exp(s_band - m)
        p_ctx = jnp.exp(s_ctx - m)
        total = jnp.sum(p_band, axis=-1, keepdims=True) + jnp.sum(p_ctx, axis=-1, keepdims=True)
        o = (jnp.dot(p_band.astype(BF16), vb, preferred_element_type=F32)
             + jnp.dot(p_ctx.astype(BF16), v_ctx, preferred_element_type=F32)) / total
        o_ref[0, pl.ds(pl.multiple_of(r * w, w), w), :] = unstack(o).astype(BF16)

    def row_group(g, carry):
        for u in range(NAT_ROWS_PER_STEP):
            grid_row(g * NAT_ROWS_PER_STEP + u)
        return carry

    assert rows % NAT_ROWS_PER_STEP == 0
    lax.fori_loop(0, rows // NAT_ROWS_PER_STEP, row_group, 0)

    def ctx_block(cb, carry):
        start = pl.multiple_of(n_lat + cb * w, w)
        qs = stacked_queries(start)
        s = _dot_nt(qs, k_ctx) * scale
        p = jnp.exp(s - jnp.max(s, axis=-1, keepdims=True))
        o = jnp.dot(p.astype(BF16), v_ctx, preferred_element_type=F32) / jnp.sum(p, axis=-1, keepdims=True)
        o_ref[0, pl.ds(start, w), :] = unstack(o).astype(BF16)
        return carry

    lax.fori_loop(0, (n_out - n_lat) // w, ctx_block, 0)


def _nat_attention(proj, bias, n_lat, n_out):
    bsz, n_all, _ = proj.shape
    qb, kb, vb = COL_NQ // BRANCH_W, COL_NK // BRANCH_W, COL_NV // BRANCH_W
    return pl.pallas_call(
        functools.partial(_nat_kernel, n_lat=n_lat, n_all=n_all, n_out=n_out),
        grid=(bsz,),
        in_specs=[pl.BlockSpec(bias.shape, lambda b: (0, 0, 0)),
                  pl.BlockSpec((1, n_all, BRANCH_W), lambda b: (b, 0, qb)),
                  pl.BlockSpec((1, n_all, BRANCH_W), lambda b: (b, 0, kb)),
                  pl.BlockSpec((1, n_all, BRANCH_W), lambda b: (b, 0, vb))],
        out_specs=pl.BlockSpec((1, n_out, BRANCH_W), lambda b: (b, 0, 0)),
        out_shape=jax.ShapeDtypeStruct((bsz, n_out, BRANCH_W), BF16),
        compiler_params=_params("parallel"),
        name="nat_attention",
    )(bias, proj, proj, proj)


def _route(logits):
    lane = _lane_iota(logits.shape)
    lane_f = lane.astype(F32)
    far = float(LANES)
    gl = jnp.where(lane < N_GROUPS, logits, -jnp.inf)
    gmax = jnp.max(gl, axis=-1, keepdims=True)
    grp = jnp.min(jnp.where(gl == gmax, lane_f, far), axis=-1, keepdims=True)
    w_grp = 1.0 / jnp.sum(jnp.exp(gl - gmax), axis=-1, keepdims=True)
    first = N_GROUPS + EXPERTS_PER_GROUP * grp
    el = jnp.where(jnp.logical_and(lane_f >= first, lane_f < first + EXPERTS_PER_GROUP), logits, -jnp.inf)
    v1 = jnp.max(el, axis=-1, keepdims=True)
    i1 = jnp.min(jnp.where(el == v1, lane_f, far), axis=-1, keepdims=True)
    el2 = jnp.where(lane_f == i1, -jnp.inf, el)
    v2 = jnp.max(el2, axis=-1, keepdims=True)
    i2 = jnp.min(jnp.where(el2 == v2, lane_f, far), axis=-1, keepdims=True)
    t = jnp.exp(v2 - v1)
    w1 = w_grp / (1.0 + t)
    w2 = w_grp * t / (1.0 + t)
    ids = jnp.where(lane == 0, i1 - N_GROUPS, jnp.where(lane == 1, i2 - N_GROUPS, 0.0)).astype(jnp.int32)
    wts = jnp.where(lane == 0, w1, jnp.where(lane == 1, w2, 0.0))
    return ids, wts


def _merge_kernel(xa_ref, xb_ref, mod_ref, gm_ref, gf_ref, b0_ref, b1_ref, b2_ref, b3_ref, wg_ref, wb_ref, wo_ref,
                  wrh_ref, wrl_ref, br_ref, x1_ref, h2_ref, ids_ref, wts_ref, *, lat_tiles):
    d = xa_ref.shape[-1]
    x = _token_tile(xa_ref, xb_ref, lat_tiles)
    mod = lambda k: mod_ref[0, 0, k:k + 1, :]
    h = _modulated_norm(x, gm_ref[...], mod(0), mod(1)).astype(BF16)
    merged = None
    for i, b_ref in enumerate((b0_ref, b1_ref, b2_ref, b3_ref)):
        gate = jax.nn.sigmoid(jnp.dot(h, wg_ref[:, i * d:(i + 1) * d], preferred_element_type=F32))
        term = gate * jnp.dot(b_ref[0], wb_ref[i], preferred_element_type=F32)
        merged = term if merged is None else merged + term
    o = jnp.dot(merged.astype(BF16), wo_ref[...], preferred_element_type=F32)
    x1 = x + mod(2) * o
    x1_ref[0] = x1
    h2 = _modulated_norm(x1, gf_ref[...], mod(3), mod(4))
    h2_ref[0] = _pack_halves(h2)
    logits = _dot_split(h2, wrh_ref[...], wrl_ref[...]) + br_ref[...]
    ids, wts = _route(logits)
    ids_ref[0] = ids
    wts_ref[0] = wts


def _merge(tokens, n_all, modv, g_mix, g_ffn, branches, w_gate, w_branch, w_out, w_router, b_router, n_lat):
    bsz, _, d = tokens[0].shape
    tm = TOKEN_TILE
    lat_tiles = n_lat // tm
    tile = lambda width: pl.BlockSpec((1, tm, width), lambda b, t: (b, t, 0))
    full = lambda shape: pl.BlockSpec(shape, lambda b, t: (0,) * len(shape))
    return pl.pallas_call(
        functools.partial(_merge_kernel, lat_tiles=lat_tiles),
        grid=(bsz, n_all // tm),
        in_specs=[*_token_specs(tokens, tm, lat_tiles),
                  pl.BlockSpec((1, 1, 6, d), lambda b, t: (b, jnp.minimum(t // lat_tiles, 1), 0, 0)),
                  full((1, d)), full((1, d)),
                  tile(BRANCH_W), tile(BRANCH_W), tile(BRANCH_W), tile(BRANCH_W),
                  full(w_gate.shape), full(w_branch.shape), full(w_out.shape),
                  full(w_router.shape), full(w_router.shape), full(b_router.shape)],
        out_specs=[tile(d), tile(d // 2), tile(LANES), tile(LANES)],
        out_shape=[jax.ShapeDtypeStruct((bsz, n_all, d), F32), jax.ShapeDtypeStruct((bsz, n_all, d // 2), jnp.int32),
                   jax.ShapeDtypeStruct((bsz, n_all, LANES), jnp.int32),
                   jax.ShapeDtypeStruct((bsz, n_all, LANES), F32)],
        compiler_params=_params("parallel", "parallel"),
        name="merge_router",
    )(tokens[0], tokens[1], modv, g_mix, g_ffn, *branches, w_gate, w_branch, w_out, *_split_bf16(w_router), b_router)


def _slot_plan(ids):
    te = EXPERT_TILE
    flat = ids.reshape(-1)
    n_pairs = flat.shape[0]
    n_tiles = n_pairs // te + N_EXPERTS
    onehot = (flat[:, None] == jnp.arange(N_EXPERTS, dtype=jnp.int32)[None, :]).astype(jnp.int32)
    running = jnp.cumsum(onehot, axis=0)
    counts = running[-1]
    rank = jnp.sum(running * onehot, axis=1) - 1
    padded = ((counts + te - 1) // te) * te
    ends = jnp.cumsum(padded)
    offsets = ends - padded
    dest = (jnp.sum(onehot * offsets[None, :], axis=1) + rank).astype(jnp.int32)
    tile_start = jnp.arange(n_tiles, dtype=jnp.int32) * te
    tile_expert = jnp.minimum(jnp.sum((tile_start[:, None] >= ends[None, :]).astype(jnp.int32), axis=1),
                              N_EXPERTS - 1).astype(jnp.int32)
    n_used = (ends[-1] // te).astype(jnp.int32).reshape(1)
    return dest, tile_expert, n_used, n_tiles


def _row_copy(src_ref, src_row, dst_ref, dst_row, sem):
    return pltpu.make_async_copy(src_ref.at[pl.ds(src_row, 1), :], dst_ref.at[pl.ds(dst_row, 1), :], sem)


ROW_DMA_UNROLL = 8


def _wait_rows(src_ref, dst_ref, sem, n_rows):
    def wait(r, carry):
        for _ in range(ROW_DMA_UNROLL):
            _row_copy(src_ref, 0, dst_ref, 0, sem).wait()
        return carry

    lax.fori_loop(0, n_rows // ROW_DMA_UNROLL, wait, 0)


def _dispatch_kernel(dest_ref, h_ref, init_ref, out_ref, stage, stage_sems, row_sem, *, tm):
    del init_ref
    i = pl.program_id(0)
    n_steps = pl.num_programs(0)
    slot = i % 2

    def stage_copy(tile, into):
        return pltpu.make_async_copy(h_ref.at[pl.ds(tile * tm, tm), :], stage.at[into], stage_sems.at[into])

    @pl.when(i == 0)
    def _():
        stage_copy(0, 0).start()

    @pl.when(i > 0)
    def _():
        _wait_rows(stage.at[0], out_ref, row_sem, 2 * tm)

    @pl.when(i + 1 < n_steps)
    def _():
        stage_copy(i + 1, 1 - slot).start()

    stage_copy(i, slot).wait()

    def start(g, carry):
        for u in range(ROW_DMA_UNROLL):
            r = g * ROW_DMA_UNROLL + u
            for k in range(2):
                _row_copy(stage.at[slot], r, out_ref, dest_ref[0, 0, 2 * r + k], row_sem).start()
        return carry

    lax.fori_loop(0, tm // ROW_DMA_UNROLL, start, 0)

    @pl.when(i == n_steps - 1)
    def _():
        _wait_rows(stage.at[0], out_ref, row_sem, 2 * tm)


def _dispatch(dest, h2, n_slots):
    n_tok, d = h2.shape
    tm = TOKEN_TILE
    return pl.pallas_call(
        functools.partial(_dispatch_kernel, tm=tm),
        grid=(n_tok // tm,),
        in_specs=[pl.BlockSpec((1, 1, 2 * tm), lambda i: (i, 0, 0), memory_space=pltpu.SMEM),
                  pl.BlockSpec(memory_space=pl.ANY),
                  pl.BlockSpec(memory_space=pl.ANY)],
        out_specs=pl.BlockSpec(memory_space=pl.ANY),
        out_shape=jax.ShapeDtypeStruct((n_slots, d), h2.dtype),
        scratch_shapes=[pltpu.VMEM((2, tm, d), h2.dtype), pltpu.SemaphoreType.DMA((2,)),
                        pltpu.SemaphoreType.DMA(())],
        input_output_aliases={2: 0},
        compiler_params=_params("arbitrary"),
        name="moe_dispatch",
    )(dest.reshape(n_tok // tm, 1, 2 * tm), h2, jnp.zeros((n_slots, d), h2.dtype))


def _ffn_kernel(te_ref, used_ref, x_ref, wg_ref, wu_ref, wd_ref, o_ref, wg_scr, wu_scr, wd_scr):
    i = pl.program_id(0)
    used = i < used_ref[0]
    new_expert = jnp.logical_or(i == 0, te_ref[i] != te_ref[jnp.maximum(i - 1, 0)])

    @pl.when(jnp.logical_and(used, new_expert))
    def _():
        wg_scr[...] = wg_ref[0].astype(BF16)
        wu_scr[...] = wu_ref[0].astype(BF16)
        wd_scr[...] = wd_ref[0].astype(BF16)

    @pl.when(used)
    def _():
        hi, lo = _unpack_halves(x_ref[...])
        x = jnp.concatenate([hi.astype(BF16), lo.astype(BF16)], axis=1)
        g = jnp.dot(x, wg_scr[...], preferred_element_type=F32)
        u = jnp.dot(x, wu_scr[...], preferred_element_type=F32)
        a = (g * jax.nn.sigmoid(g) * u).astype(BF16)
        o_ref[...] = _pack_halves(jnp.dot(a, wd_scr[...], preferred_element_type=F32))

    @pl.when(jnp.logical_not(used))
    def _():
        o_ref[...] = jnp.zeros_like(o_ref)


def _expert_ffn(tile_expert, n_used, xs, w_gate, w_up, w_down, n_tiles):
    n_slots, half = xs.shape
    d = 2 * half
    te = EXPERT_TILE
    de = w_gate.shape[-1]
    grid_spec = pltpu.PrefetchScalarGridSpec(
        num_scalar_prefetch=2,
        grid=(n_tiles,),
        in_specs=[pl.BlockSpec((te, half), lambda i, te_ref, used: (i, 0)),
                  pl.BlockSpec((1, d, de), lambda i, te_ref, used: (te_ref[i], 0, 0)),
                  pl.BlockSpec((1, d, de), lambda i, te_ref, used: (te_ref[i], 0, 0)),
                  pl.BlockSpec((1, de, d), lambda i, te_ref, used: (te_ref[i], 0, 0))],
        out_specs=pl.BlockSpec((te, half), lambda i, te_ref, used: (i, 0)),
        scratch_shapes=[pltpu.VMEM((d, de), BF16), pltpu.VMEM((d, de), BF16), pltpu.VMEM((de, d), BF16)],
    )
    return pl.pallas_call(
        _ffn_kernel,
        grid_spec=grid_spec,
        out_shape=jax.ShapeDtypeStruct((n_slots, half), jnp.int32),
        compiler_params=_params("arbitrary"),
        name="moe_ffn",
    )(tile_expert, n_used, xs, w_gate, w_up, w_down)


def _combine_kernel(dest_ref, next_ref, wts_ref, x1_ref, mod_ref, gf_ref, ys_ref, o_ref, buf, sems, *, final):
    tm = x1_ref.shape[1]
    step = pl.program_id(0) * pl.num_programs(1) + pl.program_id(1)
    n_steps = pl.num_programs(0) * pl.num_programs(1)
    slot = step % 2

    def gather(idx_ref, into):
        def start(g, carry):
            for u in range(ROW_DMA_UNROLL):
                r = g * ROW_DMA_UNROLL + u
                for k in range(2):
                    _row_copy(ys_ref, idx_ref[0, 0, 2 * r + k], buf.at[into, k], r, sems.at[into]).start()
            return carry

        lax.fori_loop(0, tm // ROW_DMA_UNROLL, start, 0)

    @pl.when(step == 0)
    def _():
        gather(dest_ref, 0)

    @pl.when(step + 1 < n_steps)
    def _():
        gather(next_ref, 1 - slot)

    _wait_rows(ys_ref, buf.at[slot, 0], sems.at[slot], 2 * tm)
    wts = wts_ref[0]
    parts = []
    for k in range(2):
        hi, lo = _unpack_halves(buf[slot, k])
        parts.append(wts[:, k:k + 1] * jnp.concatenate([hi, lo], axis=1))
    x2 = x1_ref[0] + mod_ref[0, 0, 5:6, :] * (parts[0] + parts[1])
    if final:
        ms = jnp.mean(x2 * x2, axis=-1, keepdims=True)
        x2 = x2 * lax.rsqrt(ms + RMS_EPS) * gf_ref[...]
    o_ref[0] = x2


def _combine(dest, wts, x1, modv, g_final, ys, n_lat, final):
    bsz, n_all, d = x1.shape
    tm = TOKEN_TILE
    lat_tiles = n_lat // tm
    all_tiles = n_all // tm
    tiles = lat_tiles if final else all_tiles
    n_out = n_lat if final else n_all

    def dest_row(step):
        step = jnp.minimum(step, bsz * tiles - 1)
        return (step // tiles) * all_tiles + step % tiles

    idx_spec = lambda ahead: pl.BlockSpec((1, 1, 2 * tm), lambda b, t: (dest_row(b * tiles + t + ahead), 0, 0),
                                          memory_space=pltpu.SMEM)
    dest3 = dest.reshape(bsz * all_tiles, 1, 2 * tm)
    return pl.pallas_call(
        functools.partial(_combine_kernel, final=final),
        grid=(bsz, tiles),
        in_specs=[idx_spec(0), idx_spec(1),
                  pl.BlockSpec((1, tm, LANES), lambda b, t: (b, t, 0)),
                  pl.BlockSpec((1, tm, d), lambda b, t: (b, t, 0)),
                  pl.BlockSpec((1, 1, 6, d), lambda b, t: (b, jnp.minimum(t // lat_tiles, 1), 0, 0)),
                  pl.BlockSpec((1, d), lambda b, t: (0, 0)),
                  pl.BlockSpec(memory_space=pl.ANY)],
        out_specs=pl.BlockSpec((1, tm, d), lambda b, t: (b, t, 0)),
        out_shape=jax.ShapeDtypeStruct((bsz, n_out, d), F32),
        scratch_shapes=[pltpu.VMEM((2, 2, tm, d // 2), jnp.int32), pltpu.SemaphoreType.DMA((2,))],
        compiler_params=_params("arbitrary", "arbitrary"),
        name="moe_combine",
    )(dest3, dest3, wts, x1, modv, g_final, ys)


def kernel(x, c, ctx, c_ctx, w_mod, b_mod, g_mix, g_ffn, w_in, pool_w, pool_scale, diff_lambda, diff_norm_g, nat_rpb, gqa_q_norm, gqa_k_norm, w_branch, w_out, w_router_group, b_router_group, w_router_expert, b_router_expert, w_exp_gate, w_exp_up, w_exp_down, g_final):
    bsz, n_lat, d = x.shape
    n_ctx = ctx.shape[1]
    n_all = n_lat + n_ctx
    depth = w_mod.shape[0]
    tm = TOKEN_TILE
    assert n_lat % tm == 0 and n_ctx % tm == 0 and n_lat % GRID_W == 0 and n_lat // GRID_W >= NAT_WIN_R
    assert d == w_in.shape[1] and w_in.shape[2] == PROJ_COLS + N_BRANCH * d

    tokens = (x, ctx, 0)
    mod_rows = 16
    assert bsz + 1 <= mod_rows
    cc = jnp.zeros((mod_rows, d), F32).at[:bsz].set(c).at[bsz].set(c_ctx)
    mod = _modulation(cc, w_mod, b_mod)
    mod_lat = mod[:, :bsz].reshape(depth, bsz, 1, 6, d)
    mod_ctx = jnp.broadcast_to(mod[:, bsz].reshape(depth, 1, 1, 6, d), (depth, bsz, 1, 6, d))
    modv = jnp.concatenate([mod_lat, mod_ctx], axis=2)

    tabs = _rope_tables(n_lat, n_ctx, 256, DIFF_DQ) + _rope_tables(n_lat, n_ctx, 256, HEAD_DIM)
    head_of = np.arange(LANES) // HEAD_DIM
    ones_bd = jnp.asarray((head_of[:, None] == head_of[None, :]).astype(np.float32) / HEAD_DIM, BF16)
    group_of = np.arange(BRANCH_W) // POOL_GW
    pool_mask = jnp.asarray(group_of[:, None] == group_of[None, :])

    out = None
    for l in range(depth):
        final = l == depth - 1
        n_out = n_lat if final else n_all
        w_proj = w_in[l, :, :PROJ_COLS].astype(BF16)
        w_gate = w_in[l, :, PROJ_COLS:].astype(BF16)
        tile2 = lambda v: jnp.tile(v.reshape(1, HEAD_DIM), (1, LANES // HEAD_DIM))
        u, proj = _inproj(tokens, n_all, modv[l], g_mix[l].reshape(1, d), w_proj, tabs, ones_bd,
                          tile2(gqa_q_norm[l]), tile2(gqa_k_norm[l]), n_lat)
        pool_bd = jnp.where(pool_mask, jnp.tile(pool_w[l].reshape(BRANCH_W, POOL_GW), (1, BRANCH_W // POOL_GW)),
                            0.0).astype(BF16)
        b_pool = _pool(u, pool_bd, pool_scale[l].reshape(1, BRANCH_W), n_lat, n_out)
        b_diff = _diff_attention(proj, diff_lambda[l], tile2(diff_norm_g[l]), ones_bd, n_lat, l, n_out)
        b_nat = _nat_attention(proj, _nat_bias(nat_rpb[l]), n_lat, n_out)
        b_gqa = _gqa_attention(proj, n_lat, n_out)
        n_route = N_GROUPS + N_EXPERTS
        w_router = jnp.zeros((d, LANES), F32).at[:, :N_GROUPS].set(w_router_group[l])
        w_router = w_router.at[:, N_GROUPS:n_route].set(w_router_expert[l])
        b_router = jnp.zeros((1, LANES), F32).at[0, :N_GROUPS].set(b_router_group[l])
        b_router = b_router.at[0, N_GROUPS:n_route].set(b_router_expert[l])
        x1, h2, ids, wts = _merge(tokens, n_out, modv[l], g_mix[l].reshape(1, d), g_ffn[l].reshape(1, d),
                                  (b_pool, b_diff, b_nat, b_gqa), w_gate, w_branch[l].astype(BF16),
                                  w_out[l].astype(BF16), w_router, b_router, n_lat)
        dest, tile_expert, n_used, n_tiles = _slot_plan(ids[:, :, :2])
        xs = _dispatch(dest, h2.reshape(bsz * n_out, d // 2), n_tiles * EXPERT_TILE)
        ys = _expert_ffn(tile_expert, n_used, xs, w_exp_gate[l], w_exp_up[l], w_exp_down[l], n_tiles)
        out = _combine(dest, wts, x1, modv[l], g_final.reshape(1, d), ys, n_lat, final)
        tokens = (out, out, n_lat // tm)
    return out
```

```python
import functools
import math

import numpy as np
import jax
import jax.numpy as jnp
from jax import lax
from jax.experimental import pallas as pl
from jax.experimental.pallas import tpu as pltpu

GRID_W = 64
HEAD_DIM = 64
BRANCH_W = 256
N_BRANCH = 4
POOL_HALF_WINDOWS = (1, 2, 4, 8)
POOL_GW = 64
DIFF_DQ = 32
NAT_WIN_R = 8
NAT_WIN_C = 16
N_GROUPS = 4
EXPERTS_PER_GROUP = 8
N_EXPERTS = 32
D_EXPERT = 512
ROPE_THETA = 10000.0
RMS_EPS = 1e-6
NEG_INF = -1e30
LOG2E = 1.4426950408889634

LANES = 128
TOKEN_TILE = 256
EXPERT_TILE = 256
VMEM_LIMIT_BYTES = 56 * 1024 * 1024

PROJ_COLS = 2304
COL_DQ, COL_DK, COL_DV = 0, 256, 512
COL_NQ, COL_NK, COL_NV = 768, 1024, 1280
COL_GQ, COL_GK, COL_GV = 1536, 2048, 2176

F32 = jnp.float32
BF16 = jnp.bfloat16
HIGHEST = lax.Precision.HIGHEST


def _params(*sem):
    return pltpu.CompilerParams(dimension_semantics=sem, vmem_limit_bytes=VMEM_LIMIT_BYTES)


def _lane_iota(shape):
    return lax.broadcasted_iota(jnp.int32, shape, len(shape) - 1)


def _dot_nt(a, b):
    return lax.dot_general(a, b, (((1,), (1,)), ((), ())), preferred_element_type=F32)


def _split_bf16(x):
    hi = x.astype(BF16)
    return hi, (x - hi.astype(F32)).astype(BF16)


def _dot_split(a, b_hi, b_lo=None):
    a_hi, a_lo = _split_bf16(a)
    out = jnp.dot(a_hi, b_hi, preferred_element_type=F32) + jnp.dot(a_lo, b_hi, preferred_element_type=F32)
    if b_lo is not None:
        out = out + jnp.dot(a_hi, b_lo, preferred_element_type=F32)
    return out


def _pack_halves(y):
    w = y.shape[1] // 2
    hi = pltpu.bitcast(y[:, :w].astype(BF16).astype(F32), jnp.int32)
    lo = pltpu.bitcast(y[:, w:].astype(BF16).astype(F32), jnp.int32)
    return jnp.bitwise_or(hi, lax.shift_right_logical(lo, 16))


def _unpack_halves(word):
    hi = pltpu.bitcast(jnp.bitwise_and(word, -65536), F32)
    lo = pltpu.bitcast(lax.shift_left(word, 16), F32)
    return hi, lo


def _modulated_norm(x, gain, shift, scale):
    ms = jnp.mean(x * x, axis=-1, keepdims=True)
    return (x * lax.rsqrt(ms + RMS_EPS)) * gain * (1.0 + scale) + shift


def _mod_kernel(c_ref, w_ref, b_ref, o_ref):
    c = c_ref[...]
    s = c * jax.nn.sigmoid(c)
    o_ref[0] = jnp.dot(s, w_ref[0], preferred_element_type=F32, precision=HIGHEST) + b_ref[0]


def _modulation(cc, w_mod, b_mod):
    depth, d, n6 = w_mod.shape
    rows = cc.shape[0]
    tn = 512
    return pl.pallas_call(
        _mod_kernel,
        grid=(depth, n6 // tn),
        in_specs=[pl.BlockSpec((rows, d), lambda l, j: (0, 0)),
                  pl.BlockSpec((1, d, tn), lambda l, j: (l, 0, j)),
                  pl.BlockSpec((1, 1, tn), lambda l, j: (l, 0, j))],
        out_specs=pl.BlockSpec((1, rows, tn), lambda l, j: (l, 0, j)),
        out_shape=jax.ShapeDtypeStruct((depth, rows, n6), F32),
        compiler_params=_params("arbitrary", "arbitrary"),
        name="modulation",
    )(cc, w_mod, b_mod.reshape(depth, 1, n6))


def _rope_tables(n_lat, n_ctx, width, vec_dim):
    t = np.arange(n_lat)
    row, col = t // GRID_W, t % GRID_W
    j = np.arange(width) % vec_dim
    half = vec_dim // 2
    quarter = half // 2
    jj = j % half
    inv = ROPE_THETA ** (-(jj % quarter).astype(np.float64) / quarter)
    pos = np.where((j < half)[None, :], row[:, None], col[:, None]).astype(np.float64)
    ang = pos * inv[None, :]
    sign = np.where(jj < quarter, -1.0, 1.0)[None, :]
    cos = np.concatenate([np.cos(ang), np.ones((n_ctx, width))], axis=0)
    sin = np.concatenate([np.sin(ang) * sign, np.zeros((n_ctx, width))], axis=0)
    return jnp.asarray(cos, F32), jnp.asarray(sin, F32)


def _rope(x, cos, sin, quarter):
    lane = _lane_iota(x.shape)
    first = (lane % (2 * quarter)) < quarter
    up = pltpu.roll(x, LANES - quarter, 1)
    down = pltpu.roll(x, quarter, 1)
    return x * cos + jnp.where(first, up, down) * sin


def _head_rms(x, ones_bd, gain):
    ms = _dot_split(x * x, ones_bd)
    return x * lax.rsqrt(ms + RMS_EPS) * gain


def _token_specs(tokens, tm, lat_tiles):
    lat, ctx, ctx_first = tokens
    d = lat.shape[-1]
    return [pl.BlockSpec((1, tm, d), lambda b, t: (b, jnp.minimum(t, lat_tiles - 1), 0)),
            pl.BlockSpec((1, tm, d), lambda b, t: (b, ctx_first + jnp.maximum(t - lat_tiles, 0), 0))]


def _token_tile(xa_ref, xb_ref, lat_tiles):
    return jnp.where(pl.program_id(1) >= lat_tiles, xb_ref[0], xa_ref[0])


def _inproj_kernel(xa_ref, xb_ref, mod_ref, g_ref, w_ref, cd_ref, sd_ref, cg_ref, sg_ref, ones_ref, qn_ref, kn_ref,
                   u_ref, p_ref, *, lat_tiles):
    x = _token_tile(xa_ref, xb_ref, lat_tiles)
    h = _modulated_norm(x, g_ref[...], mod_ref[0, 0, 0:1, :], mod_ref[0, 0, 1:2, :]).astype(BF16)

    def proj(col):
        y = jnp.dot(h, w_ref[:, col:col + 2 * LANES], preferred_element_type=F32)
        return y[:, 0:LANES], y[:, LANES:2 * LANES]

    for b, y in enumerate(proj(0)):
        u_ref[0, :, b * LANES:(b + 1) * LANES] = y
    for dst, src, mult in ((COL_DQ, 256, DIFF_DQ ** -0.5 * LOG2E), (COL_DK, 512, 1.0)):
        for b, y in enumerate(proj(src)):
            sl = slice(b * LANES, (b + 1) * LANES)
            y = _rope(y, cd_ref[:, sl], sd_ref[:, sl], DIFF_DQ // 4) * mult
            p_ref[0, :, dst + b * LANES:dst + (b + 1) * LANES] = y.astype(BF16)
    for dst, src in ((COL_DV, 768), (COL_NQ, 1024), (COL_NK, 1280), (COL_NV, 1536)):
        for b, y in enumerate(proj(src)):
            p_ref[0, :, dst + b * LANES:dst + (b + 1) * LANES] = y.astype(BF16)
    ones_bd = ones_ref[...]
    lane = _lane_iota((x.shape[0], LANES))
    low = lane < HEAD_DIM
    for b, y in enumerate(proj(1792)):
        sl = slice(b * LANES, (b + 1) * LANES)
        y = _head_rms(y, ones_bd, qn_ref[...])
        y = _rope(y, cg_ref[:, sl], sg_ref[:, sl], HEAD_DIM // 4) * (HEAD_DIM ** -0.5 * LOG2E)
        swapped = pltpu.roll(y, HEAD_DIM, 1)
        keep = low if b == 0 else jnp.logical_not(low)
        even = jnp.where(keep, y if b == 0 else swapped, 0.0)
        odd = jnp.where(keep, swapped if b == 0 else y, 0.0)
        p_ref[0, :, COL_GQ + (2 * b) * LANES:COL_GQ + (2 * b + 1) * LANES] = even.astype(BF16)
        p_ref[0, :, COL_GQ + (2 * b + 1) * LANES:COL_GQ + (2 * b + 2) * LANES] = odd.astype(BF16)
    gk, gv = proj(2048)
    y = _head_rms(gk, ones_bd, kn_ref[...])
    y = _rope(y, cg_ref[:, 0:LANES], sg_ref[:, 0:LANES], HEAD_DIM // 4)
    p_ref[0, :, COL_GK:COL_GK + LANES] = y.astype(BF16)
    p_ref[0, :, COL_GV:COL_GV + LANES] = gv.astype(BF16)


def _inproj(tokens, n_all, modv, g_mix, w_proj, tabs, ones_bd, q_norm, k_norm, n_lat):
    bsz, _, d = tokens[0].shape
    tm = TOKEN_TILE
    lat_tiles = n_lat // tm
    cd, sd, cg, sg = tabs
    full = lambda shape: pl.BlockSpec(shape, lambda b, t: (0,) * len(shape))
    tab = pl.BlockSpec((tm, 256), lambda b, t: (t, 0))
    return pl.pallas_call(
        functools.partial(_inproj_kernel, lat_tiles=lat_tiles),
        grid=(bsz, n_all // tm),
        in_specs=[*_token_specs(tokens, tm, lat_tiles),
                  pl.BlockSpec((1, 1, 6, d), lambda b, t: (b, jnp.minimum(t // lat_tiles, 1), 0, 0)),
                  full((1, d)), full((d, PROJ_COLS)), tab, tab, tab, tab,
                  full((LANES, LANES)), full((1, LANES)), full((1, LANES))],
        out_specs=[pl.BlockSpec((1, tm, BRANCH_W), lambda b, t: (b, t, 0)),
                   pl.BlockSpec((1, tm, PROJ_COLS), lambda b, t: (b, t, 0))],
        out_shape=[jax.ShapeDtypeStruct((bsz, n_all, BRANCH_W), F32),
                   jax.ShapeDtypeStruct((bsz, n_all, PROJ_COLS), BF16)],
        compiler_params=_params("parallel", "parallel"),
        name="inproj",
    )(tokens[0], tokens[1], modv, g_mix, w_proj, cd, sd, cg, sg, ones_bd, q_norm, k_norm)


def _pool_kernel(prev_ref, u_ref, next_ref, w_ref, scale_ref, o_ref, pad_ref, *, n_lat, n_all):
    t = pl.program_id(1)
    tm = TOKEN_TILE
    lat_tiles = n_lat // tm
    all_tiles = n_all // tm
    halo = 8
    u = u_ref[0]
    first = jnp.logical_or(t == 0, t == lat_tiles)
    last = jnp.logical_or(t == lat_tiles - 1, t == all_tiles - 1)
    pad_ref[0:halo, :] = jnp.where(first, 0.0, prev_ref[0])
    pad_ref[halo:halo + tm, :] = u
    pad_ref[halo + tm:2 * halo + tm, :] = jnp.where(last, 0.0, next_ref[0])

    def shifted(d):
        return pad_ref[halo + d:halo + d + tm, :]

    sums = []
    acc = None
    for hw in POOL_HALF_WINDOWS:
        lo = hw // 2 if acc is not None else 0
        for d in range(lo, hw):
            term = shifted(d) + shifted(-d - 1)
            acc = term if acc is None else acc + term
        sums.append(acc)
    lane = _lane_iota((tm, BRANCH_W))
    group = lane // POOL_GW
    win = jnp.where(group == 0, sums[0], jnp.where(group == 1, sums[1], jnp.where(group == 2, sums[2], sums[3])))
    half = jnp.where(group == 0, 1, jnp.where(group == 1, 2, jnp.where(group == 2, 4, 8)))
    seg_start = jnp.where(t >= lat_tiles, n_lat, 0)
    seg_len = jnp.where(t >= lat_tiles, n_all - n_lat, n_lat)
    pos = t * tm - seg_start + lax.broadcasted_iota(jnp.int32, (tm, BRANCH_W), 0)
    cnt = jnp.minimum(pos + half, seg_len) - jnp.maximum(pos - half, 0)
    pooled = (win / cnt.astype(F32) - u).astype(BF16)
    y = jnp.dot(pooled, w_ref[...], preferred_element_type=F32) * scale_ref[...]
    o_ref[0] = y.astype(BF16)


def _pool(u, w_bd, pool_scale, n_lat, n_out):
    bsz, n_all, ch = u.shape
    tm = TOKEN_TILE
    halo = 8
    per = tm // halo
    nblk = n_all // halo
    return pl.pallas_call(
        functools.partial(_pool_kernel, n_lat=n_lat, n_all=n_all),
        grid=(bsz, n_out // tm),
        in_specs=[pl.BlockSpec((1, halo, ch), lambda b, t: (b, jnp.maximum(t * per - 1, 0), 0)),
                  pl.BlockSpec((1, tm, ch), lambda b, t: (b, t, 0)),
                  pl.BlockSpec((1, halo, ch), lambda b, t: (b, jnp.minimum((t + 1) * per, nblk - 1), 0)),
                  pl.BlockSpec((ch, ch), lambda b, t: (0, 0)),
                  pl.BlockSpec((1, ch), lambda b, t: (0, 0))],
        out_specs=pl.BlockSpec((1, tm, ch), lambda b, t: (b, t, 0)),
        out_shape=jax.ShapeDtypeStruct((bsz, n_out, ch), BF16),
        scratch_shapes=[pltpu.VMEM((tm + 2 * halo, ch), F32)],
        compiler_params=_params("parallel", "parallel"),
        name="pool",
    )(u, u, u, w_bd, pool_scale)


ATTN_Q_TILE = 256
DIFF_Q_TILE = 256
ATTN_ROW_BLOCK = 64
GQA_K_TILE = 1024
DIFF_K_TILE = 2048


def _flash_scratch(rows, n_all, n_lat, tk, n_value_sets):
    wide = lambda: pltpu.VMEM((rows, LANES), F32)
    return [pltpu.VMEM((rows, LANES), BF16), pltpu.VMEM((2, rows, tk), F32),
            pltpu.VMEM((rows, n_all - n_lat), F32), pltpu.VMEM((rows, tk), BF16), wide(), wide(), wide(),
            pltpu.VMEM((n_value_sets, n_all, LANES), BF16)]


def _values_with_ones(v, own_half_mask):
    return jnp.where(own_half_mask, v, jnp.ones_like(v))


def _flash_attention(k_ref, scratch, value_rows, is_latent_block, n_lat, n_all, replicated_stats):
    q_scr, s_scr, sc_scr, p_scr, m_scr, a_scr, acc_scr, v_scr = scratch
    rows = q_scr.shape[0]
    tk = p_scr.shape[1]
    rb = ATTN_ROW_BLOCK
    n_chunks = n_lat // tk
    n_ctx = n_all - n_lat
    assert n_chunks % 2 == 0 and n_chunks * tk == n_lat and n_ctx <= tk

    m_scr[...] = jnp.full(m_scr.shape, -jnp.inf, F32)
    acc_scr[...] = jnp.zeros(acc_scr.shape, F32)

    def produce(dst, start, size):
        dst[...] = _dot_nt(q_scr[...], k_ref[0, pl.ds(start, size), :])

    def consume(src, start, size):
        for r in range(rows // rb):
            rs = slice(r * rb, (r + 1) * rb)
            s = src[rs, :]
            if replicated_stats:
                m_old = m_scr[rs, :]
                m_new = jnp.maximum(m_old, jnp.broadcast_to(jnp.max(s, axis=-1, keepdims=True), (rb, LANES)))
                m_wide = m_new
            else:
                m_old = m_scr[rs, 0:1]
                m_new = jnp.maximum(m_old, jnp.max(s, axis=-1, keepdims=True))
                m_wide = jnp.broadcast_to(m_new, (rb, LANES))
            alpha = jnp.exp2(m_old - m_new)
            for b in range(size // LANES):
                cols = slice(b * LANES, (b + 1) * LANES)
                p_scr[rs, cols] = jnp.exp2(s[:, cols] - m_new).astype(BF16)
            m_scr[rs, :] = m_wide
            a_scr[rs, :] = jnp.broadcast_to(alpha, (rb, LANES))
        for r0, r1, vset in value_rows:
            pv = jnp.dot(p_scr[r0:r1, 0:size], v_scr[vset, pl.ds(start, size), :], preferred_element_type=F32)
            acc_scr[r0:r1, :] = a_scr[r0:r1, :] * acc_scr[r0:r1, :] + pv

    def lat(chunk):
        return pl.multiple_of(chunk * tk, tk)

    produce(sc_scr, n_lat, n_ctx)

    @pl.when(is_latent_block)
    def _():
        produce(s_scr.at[0], 0, tk)

        def step(i, carry):
            produce(s_scr.at[1], lat(2 * i + 1), tk)
            consume(s_scr.at[0], lat(2 * i), tk)
            produce(s_scr.at[0], lat(2 * i + 2), tk)
            consume(s_scr.at[1], lat(2 * i + 1), tk)
            return carry

        lax.fori_loop(0, n_chunks // 2 - 1, step, 0)
        produce(s_scr.at[1], (n_chunks - 1) * tk, tk)
        consume(s_scr.at[0], (n_chunks - 2) * tk, tk)
        consume(s_scr.at[1], (n_chunks - 1) * tk, tk)

    consume(sc_scr, n_lat, n_ctx)
    acc = acc_scr[...]
    return acc / pltpu.roll(acc, HEAD_DIM, 1)


def _diff_kernel(lam_ref, g_ref, ones_ref, q_ref, k_ref, v_ref, o_ref, *scratch, n_lat, n_all, tq, lam_init):
    i = pl.program_id(2)
    lf = lam_ref[...]
    lam = (jnp.exp(jnp.sum(lf[0:1] * lf[1:2], axis=-1, keepdims=True))
           - jnp.exp(jnp.sum(lf[2:3] * lf[3:4], axis=-1, keepdims=True)) + lam_init)
    q = q_ref[0]
    lane = _lane_iota((tq, LANES))
    q_scr, v_scr = scratch[0], scratch[-1]
    for n in range(LANES // DIFF_DQ):
        start = n * DIFF_DQ
        mine = jnp.logical_and(lane >= start, lane < start + DIFF_DQ)
        q_scr[n * tq:(n + 1) * tq, :] = jnp.where(mine, q, jnp.zeros_like(q))

    @pl.when(i == 0)
    def _():
        v = v_ref[0]
        low = _lane_iota(v.shape) < HEAD_DIM
        v_scr[0] = _values_with_ones(v, low)
        v_scr[1] = _values_with_ones(v, jnp.logical_not(low))

    out = _flash_attention(k_ref, scratch, [(0, 2 * tq, 0), (2 * tq, 4 * tq, 1)], i * tq < n_lat, n_lat, n_all,
                           replicated_stats=False)
    o = jnp.where(lane < HEAD_DIM, out[0:tq] - lam * out[tq:2 * tq], out[2 * tq:3 * tq] - lam * out[3 * tq:4 * tq])
    ms = _dot_split(o * o, ones_ref[...])
    o_ref[0] = (o * lax.rsqrt(ms + RMS_EPS) * g_ref[...] * (1.0 - lam_init)).astype(BF16)


def _diff_attention(proj, lam_p, norm_g, ones_bd, n_lat, layer, n_out):
    bsz, n_all, _ = proj.shape
    tq = DIFF_Q_TILE
    lam_init = 0.8 - 0.6 * math.exp(-0.3 * layer)
    qb, kb, vb = COL_DQ // LANES, COL_DK // LANES, COL_DV // LANES
    return pl.pallas_call(
        functools.partial(_diff_kernel, n_lat=n_lat, n_all=n_all, tq=tq, lam_init=lam_init),
        grid=(bsz, 2, n_out // tq),
        in_specs=[pl.BlockSpec(lam_p.shape, lambda b, p, i: (0, 0)),
                  pl.BlockSpec((1, LANES), lambda b, p, i: (0, 0)),
                  pl.BlockSpec((LANES, LANES), lambda b, p, i: (0, 0)),
                  pl.BlockSpec((1, tq, LANES), lambda b, p, i: (b, i, qb + p)),
                  pl.BlockSpec((1, n_all, LANES), lambda b, p, i: (b, 0, kb + p)),
                  pl.BlockSpec((1, n_all, LANES), lambda b, p, i: (b, 0, vb + p))],
        out_specs=pl.BlockSpec((1, tq, LANES), lambda b, p, i: (b, i, p)),
        out_shape=jax.ShapeDtypeStruct((bsz, n_out, BRANCH_W), BF16),
        scratch_shapes=_flash_scratch(4 * tq, n_all, n_lat, min(DIFF_K_TILE, n_lat // 2), 2),
        compiler_params=_params("parallel", "parallel", "arbitrary"),
        name="diff_attention",
    )(lam_p, norm_g, ones_bd, proj, proj, proj)


def _gqa_kernel(q_ref, k_ref, v_ref, o_ref, *scratch, n_lat, n_all, tq):
    kv = pl.program_id(1)
    i = pl.program_id(2)
    q_scr, v_scr = scratch[0], scratch[-1]
    for g in range(2):
        q_scr[g * tq:(g + 1) * tq, :] = q_ref[0, :, g * LANES:(g + 1) * LANES]

    @pl.when(i == 0)
    def _():
        v = v_ref[0]
        head_of_lane = _lane_iota(v.shape) // HEAD_DIM
        v_scr[0] = _values_with_ones(v, head_of_lane == kv)

    out = _flash_attention(k_ref, scratch, [(0, 2 * tq, 0)], i * tq < n_lat, n_lat, n_all, replicated_stats=True)
    outs = [out[0:tq], out[tq:2 * tq]]
    placed = [jnp.where(kv == g, outs[g], pltpu.roll(outs[g], HEAD_DIM, 1)) for g in range(2)]
    lane = _lane_iota((tq, LANES))
    o_ref[0] = jnp.where(lane < HEAD_DIM, placed[0], placed[1]).astype(BF16)


def _gqa_attention(proj, n_lat, n_out):
    bsz, n_all, _ = proj.shape
    tq = ATTN_Q_TILE
    qb, kb, vb = COL_GQ // (2 * LANES), COL_GK // LANES, COL_GV // LANES
    return pl.pallas_call(
        functools.partial(_gqa_kernel, n_lat=n_lat, n_all=n_all, tq=tq),
        grid=(bsz, 2, n_out // tq),
        in_specs=[pl.BlockSpec((1, tq, 2 * LANES), lambda b, p, i: (b, i, qb + p)),
                  pl.BlockSpec((1, n_all, LANES), lambda b, p, i: (b, 0, kb)),
                  pl.BlockSpec((1, n_all, LANES), lambda b, p, i: (b, 0, vb))],
        out_specs=pl.BlockSpec((1, tq, LANES), lambda b, p, i: (b, i, p)),
        out_shape=jax.ShapeDtypeStruct((bsz, n_out, BRANCH_W), BF16),
        scratch_shapes=_flash_scratch(2 * tq, n_all, n_lat, min(GQA_K_TILE, n_lat // 2), 1),
        compiler_params=_params("parallel", "parallel", "arbitrary"),
        name="gqa_attention",
    )(proj, proj, proj)


def _nat_bias(rpb):
    kc = NAT_WIN_C
    j = np.arange(GRID_W)
    col_start = np.clip(j - kc // 2, 0, GRID_W - kc)
    valid = (j[None, :] >= col_start[:, None]) & (j[None, :] < col_start[:, None] + kc)
    dc = np.clip(j[None, :] - j[:, None], -(kc - 1), kc - 1) + (kc - 1)
    tab = rpb[:, :, dc]
    tab = jnp.where(jnp.asarray(valid)[None, None], tab.astype(F32), NEG_INF)
    heads = rpb.shape[0]
    slabs = []
    for dr0 in range(NAT_WIN_R):
        blk = tab[:, dr0:dr0 + NAT_WIN_R]
        slabs.append(blk.transpose(0, 2, 1, 3).reshape(heads * GRID_W, NAT_WIN_R * GRID_W))
    return jnp.stack(slabs)


NAT_ROWS_PER_STEP = 4


def _nat_kernel(bias_ref, q_ref, k_ref, v_ref, o_ref, *, n_lat, n_all, n_out):
    w = GRID_W
    rows = n_lat // w
    heads = BRANCH_W // HEAD_DIM
    scale = HEAD_DIM ** -0.5
    lane = _lane_iota((w, BRANCH_W))
    head_of_lane = lane // HEAD_DIM
    k_ctx = k_ref[0, n_lat:n_all, :]
    v_ctx = v_ref[0, n_lat:n_all, :]

    def stacked_queries(start):
        q = q_ref[0, pl.ds(start, w), :]
        return jnp.concatenate([jnp.where(head_of_lane == h, q, jnp.zeros_like(q)) for h in range(heads)], axis=0)

    def unstack(o):
        out = jnp.zeros((w, BRANCH_W), F32)
        for h in range(heads):
            out = jnp.where(head_of_lane == h, o[h * w:(h + 1) * w, :], out)
        return out

    def grid_row(r):
        r0 = jnp.clip(r - NAT_WIN_R // 2, 0, rows - NAT_WIN_R)
        dr0 = r0 - r + (NAT_WIN_R - 1)
        qs = stacked_queries(pl.multiple_of(r * w, w))
        kb = k_ref[0, pl.ds(pl.multiple_of(r0 * w, w), NAT_WIN_R * w), :]
        vb = v_ref[0, pl.ds(pl.multiple_of(r0 * w, w), NAT_WIN_R * w), :]
        s_band = _dot_nt(qs, kb) * scale + bias_ref[dr0]
        s_ctx = _dot_nt(qs, k_ctx) * scale
        m = jnp.maximum(jnp.max(s_band, axis=-1, keepdims=True), jnp.max(s_ctx, axis=-1, keepdims=True))
        p_band = jnp.exp(s_band - m)
        p_ctx = jnp.exp(s_ctx - m)
        total = jnp.sum(p_band, axis=-1, keepdims=True) + jnp.sum(p_ctx, axis=-1, keepdims=True)
        o = (jnp.dot(p_band.astype(BF16), vb, preferred_element_type=F32)
             + jnp.dot(p_ctx.astype(BF16), v_ctx, preferred_element_type=F32)) / total
        o_ref[0, pl.ds(pl.multiple_of(r * w, w), w), :] = unstack(o).astype(BF16)

    def row_group(g, carry):
        for u in range(NAT_ROWS_PER_STEP):
            grid_row(g * NAT_ROWS_PER_STEP + u)
        return carry

    assert rows % NAT_ROWS_PER_STEP == 0
    lax.fori_loop(0, rows // NAT_ROWS_PER_STEP, row_group, 0)

    def ctx_block(cb, carry):
        start = pl.multiple_of(n_lat + cb * w, w)
        qs = stacked_queries(start)
        s = _dot_nt(qs, k_ctx) * scale
        p = jnp.exp(s - jnp.max(s, axis=-1, keepdims=True))
        o = jnp.dot(p.astype(BF16), v_ctx, preferred_element_type=F32) / jnp.sum(p, axis=-1, keepdims=True)
        o_ref[0, pl.ds(start, w), :] = unstack(o).astype(BF16)
        return carry

    lax.fori_loop(0, (n_out - n_lat) // w, ctx_block, 0)


def _nat_attention(proj, bias, n_lat, n_out):
    bsz, n_all, _ = proj.shape
    qb, kb, vb = COL_NQ // BRANCH_W, COL_NK // BRANCH_W, COL_NV // BRANCH_W
    return pl.pallas_call(
        functools.partial(_nat_kernel, n_lat=n_lat, n_all=n_all, n_out=n_out),
        grid=(bsz,),
        in_specs=[pl.BlockSpec(bias.shape, lambda b: (0, 0, 0)),
                  pl.BlockSpec((1, n_all, BRANCH_W), lambda b: (b, 0, qb)),
                  pl.BlockSpec((1, n_all, BRANCH_W), lambda b: (b, 0, kb)),
                  pl.BlockSpec((1, n_all, BRANCH_W), lambda b: (b, 0, vb))],
        out_specs=pl.BlockSpec((1, n_out, BRANCH_W), lambda b: (b, 0, 0)),
        out_shape=jax.ShapeDtypeStruct((bsz, n_out, BRANCH_W), BF16),
        compiler_params=_params("parallel"),
        name="nat_attention",
    )(bias, proj, proj, proj)


def _route(logits):
    lane = _lane_iota(logits.shape)
    lane_f = lane.astype(F32)
    far = float(LANES)
    gl = jnp.where(lane < N_GROUPS, logits, -jnp.inf)
    gmax = jnp.max(gl, axis=-1, keepdims=True)
    grp = jnp.min(jnp.where(gl == gmax, lane_f, far), axis=-1, keepdims=True)
    w_grp = 1.0 / jnp.sum(jnp.exp(gl - gmax), axis=-1, keepdims=True)
    first = N_GROUPS + EXPERTS_PER_GROUP * grp
    el = jnp.where(jnp.logical_and(lane_f >= first, lane_f < first + EXPERTS_PER_GROUP), logits, -jnp.inf)
    v1 = jnp.max(el, axis=-1, keepdims=True)
    i1 = jnp.min(jnp.where(el == v1, lane_f, far), axis=-1, keepdims=True)
    el2 = jnp.where(lane_f == i1, -jnp.inf, el)
    v2 = jnp.max(el2, axis=-1, keepdims=True)
    i2 = jnp.min(jnp.where(el2 == v2, lane_f, far), axis=-1, keepdims=True)
    t = jnp.exp(v2 - v1)
    w1 = w_grp / (1.0 + t)
    w2 = w_grp * t / (1.0 + t)
    ids = jnp.where(lane == 0, i1 - N_GROUPS, jnp.where(lane == 1, i2 - N_GROUPS, 0.0)).astype(jnp.int32)
    wts = jnp.where(lane == 0, w1, jnp.where(lane == 1, w2, 0.0))
    return ids, wts


def _merge_kernel(xa_ref, xb_ref, mod_ref, gm_ref, gf_ref, b0_ref, b1_ref, b2_ref, b3_ref, wg_ref, wb_ref, wo_ref,
                  wrh_ref, wrl_ref, br_ref, x1_ref, h2_ref, ids_ref, wts_ref, *, lat_tiles):
    d = xa_ref.shape[-1]
    x = _token_tile(xa_ref, xb_ref, lat_tiles)
    mod = lambda k: mod_ref[0, 0, k:k + 1, :]
    h = _modulated_norm(x, gm_ref[...], mod(0), mod(1)).astype(BF16)
    merged = None
    for i, b_ref in enumerate((b0_ref, b1_ref, b2_ref, b3_ref)):
        gate = jax.nn.sigmoid(jnp.dot(h, wg_ref[:, i * d:(i + 1) * d], preferred_element_type=F32))
        term = gate * jnp.dot(b_ref[0], wb_ref[i], preferred_element_type=F32)
        merged = term if merged is None else merged + term
    o = jnp.dot(merged.astype(BF16), wo_ref[...], preferred_element_type=F32)
    x1 = x + mod(2) * o
    x1_ref[0] = x1
    h2 = _modulated_norm(x1, gf_ref[...], mod(3), mod(4))
    h2_ref[0] = _pack_halves(h2)
    logits = _dot_split(h2, wrh_ref[...], wrl_ref[...]) + br_ref[...]
    ids, wts = _route(logits)
    ids_ref[0] = ids
    wts_ref[0] = wts


def _merge(tokens, n_all, modv, g_mix, g_ffn, branches, w_gate, w_branch, w_out, w_router, b_router, n_lat):
    bsz, _, d = tokens[0].shape
    tm = TOKEN_TILE
    lat_tiles = n_lat // tm
    tile = lambda width: pl.BlockSpec((1, tm, width), lambda b, t: (b, t, 0))
    full = lambda shape: pl.BlockSpec(shape, lambda b, t: (0,) * len(shape))
    return pl.pallas_call(
        functools.partial(_merge_kernel, lat_tiles=lat_tiles),
        grid=(bsz, n_all // tm),
        in_specs=[*_token_specs(tokens, tm, lat_tiles),
                  pl.BlockSpec((1, 1, 6, d), lambda b, t: (b, jnp.minimum(t // lat_tiles, 1), 0, 0)),
                  full((1, d)), full((1, d)),
                  tile(BRANCH_W), tile(BRANCH_W), tile(BRANCH_W), tile(BRANCH_W),
                  full(w_gate.shape), full(w_branch.shape), full(w_out.shape),
                  full(w_router.shape), full(w_router.shape), full(b_router.shape)],
        out_specs=[tile(d), tile(d // 2), tile(LANES), tile(LANES)],
        out_shape=[jax.ShapeDtypeStruct((bsz, n_all, d), F32), jax.ShapeDtypeStruct((bsz, n_all, d // 2), jnp.int32),
                   jax.ShapeDtypeStruct((bsz, n_all, LANES), jnp.int32),
                   jax.ShapeDtypeStruct((bsz, n_all, LANES), F32)],
        compiler_params=_params("parallel", "parallel"),
        name="merge_router",
    )(tokens[0], tokens[1], modv, g_mix, g_ffn, *branches, w_gate, w_branch, w_out, *_split_bf16(w_router), b_router)


def _slot_plan(ids):
    te = EXPERT_TILE
    flat = ids.reshape(-1)
    n_pairs = flat.shape[0]
    n_tiles = n_pairs // te + N_EXPERTS
    onehot = (flat[:, None] == jnp.arange(N_EXPERTS, dtype=jnp.int32)[None, :]).astype(jnp.int32)
    running = jnp.cumsum(onehot, axis=0)
    counts = running[-1]
    rank = jnp.sum(running * onehot, axis=1) - 1
    padded = ((counts + te - 1) // te) * te
    ends = jnp.cumsum(padded)
    offsets = ends - padded
    dest = (jnp.sum(onehot * offsets[None, :], axis=1) + rank).astype(jnp.int32)
    tile_start = jnp.arange(n_tiles, dtype=jnp.int32) * te
    tile_expert = jnp.minimum(jnp.sum((tile_start[:, None] >= ends[None, :]).astype(jnp.int32), axis=1),
                              N_EXPERTS - 1).astype(jnp.int32)
    n_used = (ends[-1] // te).astype(jnp.int32).reshape(1)
    return dest, tile_expert, n_used, n_tiles


SUBLANES = 8
ROW_DMA_UNROLL = SUBLANES


def _hbm_row(ref, row):
    return ref.at[pl.ds(row, 1), :]


def _vmem_row(ref, group, sub):
    return ref.at[group, pl.ds(sub, 1), :]


def _wait_rows(make_copy, n_rows):
    def wait(r, carry):
        for _ in range(ROW_DMA_UNROLL):
            make_copy().wait()
        return carry

    lax.fori_loop(0, n_rows // ROW_DMA_UNROLL, wait, 0)


def _dispatch_kernel(dest_ref, h_ref, init_ref, out_ref, stage, stage_sems, row_sem, *, tm):
    del init_ref
    i = pl.program_id(0)
    n_steps = pl.num_programs(0)
    slot = i % 2
    groups = tm // SUBLANES

    def stage_copy(tile, into):
        return pltpu.make_async_copy(h_ref.at[pl.ds(tile * groups, groups)], stage.at[into], stage_sems.at[into])

    def row_copy(group, sub, dst_row):
        return pltpu.make_async_copy(_vmem_row(stage.at[slot], group, sub), _hbm_row(out_ref, dst_row), row_sem)

    wait_all = functools.partial(_wait_rows, lambda: row_copy(0, 0, 0), 2 * tm)

    @pl.when(i == 0)
    def _():
        stage_copy(0, 0).start()

    pl.when(i > 0)(wait_all)

    @pl.when(i + 1 < n_steps)
    def _():
        stage_copy(i + 1, 1 - slot).start()

    stage_copy(i, slot).wait()

    def start(g, carry):
        for u in range(SUBLANES):
            for k in range(2):
                row_copy(g, u, dest_ref[0, 0, 2 * (g * SUBLANES + u) + k]).start()
        return carry

    lax.fori_loop(0, groups, start, 0)
    pl.when(i == n_steps - 1)(wait_all)


def _dispatch(dest, h2, n_slots):
    n_tok, d = h2.shape
    tm = TOKEN_TILE
    groups = tm // SUBLANES
    return pl.pallas_call(
        functools.partial(_dispatch_kernel, tm=tm),
        grid=(n_tok // tm,),
        in_specs=[pl.BlockSpec((1, 1, 2 * tm), lambda i: (i, 0, 0), memory_space=pltpu.SMEM),
                  pl.BlockSpec(memory_space=pl.ANY),
                  pl.BlockSpec(memory_space=pl.ANY)],
        out_specs=pl.BlockSpec(memory_space=pl.ANY),
        out_shape=jax.ShapeDtypeStruct((n_slots, d), h2.dtype),
        scratch_shapes=[pltpu.VMEM((2, groups, SUBLANES, d), h2.dtype), pltpu.SemaphoreType.DMA((2,)),
                        pltpu.SemaphoreType.DMA(())],
        input_output_aliases={2: 0},
        compiler_params=_params("arbitrary"),
        name="moe_dispatch",
    )(dest.reshape(n_tok // tm, 1, 2 * tm), h2.reshape(n_tok // SUBLANES, SUBLANES, d),
      jnp.zeros((n_slots, d), h2.dtype))


def _ffn_kernel(te_ref, used_ref, x_ref, wg_ref, wu_ref, wd_ref, o_ref, wg_scr, wu_scr, wd_scr):
    i = pl.program_id(0)
    used = i < used_ref[0]
    new_expert = jnp.logical_or(i == 0, te_ref[i] != te_ref[jnp.maximum(i - 1, 0)])

    @pl.when(jnp.logical_and(used, new_expert))
    def _():
        wg_scr[...] = wg_ref[0].astype(BF16)
        wu_scr[...] = wu_ref[0].astype(BF16)
        wd_scr[...] = wd_ref[0].astype(BF16)

    @pl.when(used)
    def _():
        hi, lo = _unpack_halves(x_ref[...])
        x = jnp.concatenate([hi.astype(BF16), lo.astype(BF16)], axis=1)
        g = jnp.dot(x, wg_scr[...], preferred_element_type=F32)
        u = jnp.dot(x, wu_scr[...], preferred_element_type=F32)
        a = (g * jax.nn.sigmoid(g) * u).astype(BF16)
        o_ref[...] = _pack_halves(jnp.dot(a, wd_scr[...], preferred_element_type=F32))

    @pl.when(jnp.logical_not(used))
    def _():
        o_ref[...] = jnp.zeros_like(o_ref)


def _expert_ffn(tile_expert, n_used, xs, w_gate, w_up, w_down, n_tiles, layer):
    n_slots, half = xs.shape
    d = 2 * half
    te = EXPERT_TILE
    de = w_gate.shape[-1]
    grid_spec = pltpu.PrefetchScalarGridSpec(
        num_scalar_prefetch=2,
        grid=(n_tiles,),
        in_specs=[pl.BlockSpec((te, half), lambda i, te_ref, used: (i, 0)),
                  pl.BlockSpec((None, 1, d, de), lambda i, te_ref, used: (layer, te_ref[i], 0, 0)),
                  pl.BlockSpec((None, 1, d, de), lambda i, te_ref, used: (layer, te_ref[i], 0, 0)),
                  pl.BlockSpec((None, 1, de, d), lambda i, te_ref, used: (layer, te_ref[i], 0, 0))],
        out_specs=pl.BlockSpec((te, half), lambda i, te_ref, used: (i, 0)),
        scratch_shapes=[pltpu.VMEM((d, de), BF16), pltpu.VMEM((d, de), BF16), pltpu.VMEM((de, d), BF16)],
    )
    return pl.pallas_call(
        _ffn_kernel,
        grid_spec=grid_spec,
        out_shape=jax.ShapeDtypeStruct((n_slots, half), jnp.int32),
        compiler_params=_params("arbitrary"),
        name="moe_ffn",
    )(tile_expert, n_used, xs, w_gate, w_up, w_down)


def _combine_kernel(dest_ref, next_ref, wts_ref, x1_ref, mod_ref, gf_ref, ys_ref, o_ref, buf, sems, *, final):
    tm = x1_ref.shape[1]
    step = pl.program_id(0) * pl.num_programs(1) + pl.program_id(1)
    n_steps = pl.num_programs(0) * pl.num_programs(1)
    slot = step % 2

    def row_copy(src_row, into, k, group, sub):
        return pltpu.make_async_copy(_hbm_row(ys_ref, src_row), _vmem_row(buf.at[into, k], group, sub),
                                     sems.at[into])

    def gather(idx_ref, into):
        def start(g, carry):
            for u in range(SUBLANES):
                for k in range(2):
                    row_copy(idx_ref[0, 0, 2 * (g * SUBLANES + u) + k], into, k, g, u).start()
            return carry

        lax.fori_loop(0, tm // SUBLANES, start, 0)

    @pl.when(step == 0)
    def _():
        gather(dest_ref, 0)

    @pl.when(step + 1 < n_steps)
    def _():
        gather(next_ref, 1 - slot)

    _wait_rows(lambda: row_copy(0, slot, 0, 0, 0), 2 * tm)
    wts = wts_ref[0]
    parts = []
    for k in range(2):
        hi, lo = _unpack_halves(buf[slot, k].reshape(tm, buf.shape[-1]))
        parts.append(wts[:, k:k + 1] * jnp.concatenate([hi, lo], axis=1))
    x2 = x1_ref[0] + mod_ref[0, 0, 5:6, :] * (parts[0] + parts[1])
    if final:
        ms = jnp.mean(x2 * x2, axis=-1, keepdims=True)
        x2 = x2 * lax.rsqrt(ms + RMS_EPS) * gf_ref[...]
    o_ref[0] = x2


def _combine(dest, wts, x1, modv, g_final, ys, n_lat, final):
    bsz, n_all, d = x1.shape
    tm = TOKEN_TILE
    lat_tiles = n_lat // tm
    all_tiles = n_all // tm
    tiles = lat_tiles if final else all_tiles
    n_out = n_lat if final else n_all

    def dest_row(step):
        step = jnp.minimum(step, bsz * tiles - 1)
        return (step // tiles) * all_tiles + step % tiles

    idx_spec = lambda ahead: pl.BlockSpec((1, 1, 2 * tm), lambda b, t: (dest_row(b * tiles + t + ahead), 0, 0),
                                          memory_space=pltpu.SMEM)
    dest3 = dest.reshape(bsz * all_tiles, 1, 2 * tm)
    return pl.pallas_call(
        functools.partial(_combine_kernel, final=final),
        grid=(bsz, tiles),
        in_specs=[idx_spec(0), idx_spec(1),
                  pl.BlockSpec((1, tm, LANES), lambda b, t: (b, t, 0)),
                  pl.BlockSpec((1, tm, d), lambda b, t: (b, t, 0)),
                  pl.BlockSpec((1, 1, 6, d), lambda b, t: (b, jnp.minimum(t // lat_tiles, 1), 0, 0)),
                  pl.BlockSpec((1, d), lambda b, t: (0, 0)),
                  pl.BlockSpec(memory_space=pl.ANY)],
        out_specs=pl.BlockSpec((1, tm, d), lambda b, t: (b, t, 0)),
        out_shape=jax.ShapeDtypeStruct((bsz, n_out, d), F32),
        scratch_shapes=[pltpu.VMEM((2, 2, tm // SUBLANES, SUBLANES, d // 2), jnp.int32),
                        pltpu.SemaphoreType.DMA((2,))],
        compiler_params=_params("arbitrary", "arbitrary"),
        name="moe_combine",
    )(dest3, dest3, wts, x1, modv, g_final, ys)


def kernel(x, c, ctx, c_ctx, w_mod, b_mod, g_mix, g_ffn, w_in, pool_w, pool_scale, diff_lambda, diff_norm_g, nat_rpb, gqa_q_norm, gqa_k_norm, w_branch, w_out, w_router_group, b_router_group, w_router_expert, b_router_expert, w_exp_gate, w_exp_up, w_exp_down, g_final):
    bsz, n_lat, d = x.shape
    n_ctx = ctx.shape[1]
    n_all = n_lat + n_ctx
    depth = w_mod.shape[0]
    tm = TOKEN_TILE
    assert n_lat % tm == 0 and n_ctx % tm == 0 and n_lat % GRID_W == 0 and n_lat // GRID_W >= NAT_WIN_R
    assert d == w_in.shape[1] and w_in.shape[2] == PROJ_COLS + N_BRANCH * d

    tokens = (x, ctx, 0)
    mod_rows = 16
    assert bsz + 1 <= mod_rows
    cc = jnp.zeros((mod_rows, d), F32).at[:bsz].set(c).at[bsz].set(c_ctx)
    mod = _modulation(cc, w_mod, b_mod)
    mod_lat = mod[:, :bsz].reshape(depth, bsz, 1, 6, d)
    mod_ctx = jnp.broadcast_to(mod[:, bsz].reshape(depth, 1, 1, 6, d), (depth, bsz, 1, 6, d))
    modv = jnp.concatenate([mod_lat, mod_ctx], axis=2)

    tabs = _rope_tables(n_lat, n_ctx, 256, DIFF_DQ) + _rope_tables(n_lat, n_ctx, 256, HEAD_DIM)
    head_of = np.arange(LANES) // HEAD_DIM
    ones_bd = jnp.asarray((head_of[:, None] == head_of[None, :]).astype(np.float32) / HEAD_DIM, BF16)
    group_of = np.arange(BRANCH_W) // POOL_GW
    pool_mask = jnp.asarray(group_of[:, None] == group_of[None, :])

    out = None
    for l in range(depth):
        final = l == depth - 1
        n_out = n_lat if final else n_all
        w_proj = w_in[l, :, :PROJ_COLS].astype(BF16)
        w_gate = w_in[l, :, PROJ_COLS:].astype(BF16)
        tile2 = lambda v: jnp.tile(v.reshape(1, HEAD_DIM), (1, LANES // HEAD_DIM))
        u, proj = _inproj(tokens, n_all, modv[l], g_mix[l].reshape(1, d), w_proj, tabs, ones_bd,
                          tile2(gqa_q_norm[l]), tile2(gqa_k_norm[l]), n_lat)
        pool_bd = jnp.where(pool_mask, jnp.tile(pool_w[l].reshape(BRANCH_W, POOL_GW), (1, BRANCH_W // POOL_GW)),
                            0.0).astype(BF16)
        b_pool = _pool(u, pool_bd, pool_scale[l].reshape(1, BRANCH_W), n_lat, n_out)
        b_diff = _diff_attention(proj, diff_lambda[l], tile2(diff_norm_g[l]), ones_bd, n_lat, l, n_out)
        b_nat = _nat_attention(proj, _nat_bias(nat_rpb[l]), n_lat, n_out)
        b_gqa = _gqa_attention(proj, n_lat, n_out)
        n_route = N_GROUPS + N_EXPERTS
        w_router = jnp.zeros((d, LANES), F32).at[:, :N_GROUPS].set(w_router_group[l])
        w_router = w_router.at[:, N_GROUPS:n_route].set(w_router_expert[l])
        b_router = jnp.zeros((1, LANES), F32).at[0, :N_GROUPS].set(b_router_group[l])
        b_router = b_router.at[0, N_GROUPS:n_route].set(b_router_expert[l])
        x1, h2, ids, wts = _merge(tokens, n_out, modv[l], g_mix[l].reshape(1, d), g_ffn[l].reshape(1, d),
                                  (b_pool, b_diff, b_nat, b_gqa), w_gate, w_branch[l].astype(BF16),
                                  w_out[l].astype(BF16), w_router, b_router, n_lat)
        dest, tile_expert, n_used, n_tiles = _slot_plan(ids[:, :, :2])
        xs = _dispatch(dest, h2.reshape(bsz * n_out, d // 2), n_tiles * EXPERT_TILE)
        ys = _expert_ffn(tile_expert, n_used, xs, w_exp_gate, w_exp_up, w_exp_down, n_tiles, l)
        out = _combine(dest, wts, x1, modv[l], g_final.reshape(1, d), ys, n_lat, final)
        tokens = (out, out, n_lat // tm)
    return out
```

```python
import functools
import math

import numpy as np
import jax
import jax.numpy as jnp
from jax import lax
from jax.experimental import pallas as pl
from jax.experimental.pallas import tpu as pltpu

GRID_W = 64
HEAD_DIM = 64
BRANCH_W = 256
N_BRANCH = 4
POOL_HALF_WINDOWS = (1, 2, 4, 8)
POOL_GW = 64
DIFF_DQ = 32
NAT_WIN_R = 8
NAT_WIN_C = 16
N_GROUPS = 4
EXPERTS_PER_GROUP = 8
N_EXPERTS = 32
D_EXPERT = 512
ROPE_THETA = 10000.0
RMS_EPS = 1e-6
NEG_INF = -1e30
LOG2E = 1.4426950408889634

LANES = 128
TOKEN_TILE = 256
EXPERT_TILE = 256
VMEM_LIMIT_BYTES = 56 * 1024 * 1024

PROJ_COLS = 2304
COL_DQ, COL_DK, COL_DV = 0, 256, 512
COL_NQ, COL_NK, COL_NV = 768, 1024, 1280
COL_GQ, COL_GK, COL_GV = 1536, 2048, 2176

F32 = jnp.float32
BF16 = jnp.bfloat16
HIGHEST = lax.Precision.HIGHEST


def _params(*sem):
    return pltpu.CompilerParams(dimension_semantics=sem, vmem_limit_bytes=VMEM_LIMIT_BYTES)


def _lane_iota(shape):
    return lax.broadcasted_iota(jnp.int32, shape, len(shape) - 1)


def _dot_nt(a, b):
    return lax.dot_general(a, b, (((1,), (1,)), ((), ())), preferred_element_type=F32)


def _split_bf16(x):
    hi = x.astype(BF16)
    return hi, (x - hi.astype(F32)).astype(BF16)


def _dot_split(a, b_hi, b_lo=None):
    a_hi, a_lo = _split_bf16(a)
    out = jnp.dot(a_hi, b_hi, preferred_element_type=F32) + jnp.dot(a_lo, b_hi, preferred_element_type=F32)
    if b_lo is not None:
        out = out + jnp.dot(a_hi, b_lo, preferred_element_type=F32)
    return out


def _pack_halves(y):
    w = y.shape[1] // 2
    hi = pltpu.bitcast(y[:, :w].astype(BF16).astype(F32), jnp.int32)
    lo = pltpu.bitcast(y[:, w:].astype(BF16).astype(F32), jnp.int32)
    return jnp.bitwise_or(hi, lax.shift_right_logical(lo, 16))


def _unpack_halves(word):
    hi = pltpu.bitcast(jnp.bitwise_and(word, -65536), F32)
    lo = pltpu.bitcast(lax.shift_left(word, 16), F32)
    return hi, lo


def _modulated_norm(x, gain, shift, scale):
    ms = jnp.mean(x * x, axis=-1, keepdims=True)
    return (x * lax.rsqrt(ms + RMS_EPS)) * gain * (1.0 + scale) + shift


def _mod_kernel(c_ref, w_ref, b_ref, o_ref):
    c = c_ref[...]
    s = c * jax.nn.sigmoid(c)
    o_ref[0] = jnp.dot(s, w_ref[0], preferred_element_type=F32, precision=HIGHEST) + b_ref[0]


def _modulation(cc, w_mod, b_mod):
    depth, d, n6 = w_mod.shape
    rows = cc.shape[0]
    tn = 512
    return pl.pallas_call(
        _mod_kernel,
        grid=(depth, n6 // tn),
        in_specs=[pl.BlockSpec((rows, d), lambda l, j: (0, 0)),
                  pl.BlockSpec((1, d, tn), lambda l, j: (l, 0, j)),
                  pl.BlockSpec((1, 1, tn), lambda l, j: (l, 0, j))],
        out_specs=pl.BlockSpec((1, rows, tn), lambda l, j: (l, 0, j)),
        out_shape=jax.ShapeDtypeStruct((depth, rows, n6), F32),
        compiler_params=_params("arbitrary", "arbitrary"),
        name="modulation",
    )(cc, w_mod, b_mod.reshape(depth, 1, n6))


def _rope_tables(n_lat, n_ctx, width, vec_dim):
    t = np.arange(n_lat)
    row, col = t // GRID_W, t % GRID_W
    j = np.arange(width) % vec_dim
    half = vec_dim // 2
    quarter = half // 2
    jj = j % half
    inv = ROPE_THETA ** (-(jj % quarter).astype(np.float64) / quarter)
    pos = np.where((j < half)[None, :], row[:, None], col[:, None]).astype(np.float64)
    ang = pos * inv[None, :]
    sign = np.where(jj < quarter, -1.0, 1.0)[None, :]
    cos = np.concatenate([np.cos(ang), np.ones((n_ctx, width))], axis=0)
    sin = np.concatenate([np.sin(ang) * sign, np.zeros((n_ctx, width))], axis=0)
    return jnp.asarray(cos, F32), jnp.asarray(sin, F32)


def _rope(x, cos, sin, quarter):
    lane = _lane_iota(x.shape)
    first = (lane % (2 * quarter)) < quarter
    up = pltpu.roll(x, LANES - quarter, 1)
    down = pltpu.roll(x, quarter, 1)
    return x * cos + jnp.where(first, up, down) * sin


def _head_rms(x, ones_bd, gain):
    ms = _dot_split(x * x, ones_bd)
    return x * lax.rsqrt(ms + RMS_EPS) * gain


def _token_specs(tokens, tm, lat_tiles):
    lat, ctx, ctx_first = tokens
    d = lat.shape[-1]
    return [pl.BlockSpec((1, tm, d), lambda b, t: (b, jnp.minimum(t, lat_tiles - 1), 0)),
            pl.BlockSpec((1, tm, d), lambda b, t: (b, ctx_first + jnp.maximum(t - lat_tiles, 0), 0))]


def _token_tile(xa_ref, xb_ref, lat_tiles):
    return jnp.where(pl.program_id(1) >= lat_tiles, xb_ref[0], xa_ref[0])


def _inproj_kernel(xa_ref, xb_ref, mod_ref, g_ref, w_ref, cd_ref, sd_ref, cg_ref, sg_ref, ones_ref, qn_ref, kn_ref,
                   u_ref, p_ref, *, lat_tiles):
    x = _token_tile(xa_ref, xb_ref, lat_tiles)
    h = _modulated_norm(x, g_ref[...], mod_ref[0, 0, 0:1, :], mod_ref[0, 0, 1:2, :]).astype(BF16)

    def proj(col):
        y = jnp.dot(h, w_ref[:, col:col + 2 * LANES], preferred_element_type=F32)
        return y[:, 0:LANES], y[:, LANES:2 * LANES]

    for b, y in enumerate(proj(0)):
        u_ref[0, :, b * LANES:(b + 1) * LANES] = y
    for dst, src, mult in ((COL_DQ, 256, DIFF_DQ ** -0.5 * LOG2E), (COL_DK, 512, 1.0)):
        for b, y in enumerate(proj(src)):
            sl = slice(b * LANES, (b + 1) * LANES)
            y = _rope(y, cd_ref[:, sl], sd_ref[:, sl], DIFF_DQ // 4) * mult
            p_ref[0, :, dst + b * LANES:dst + (b + 1) * LANES] = y.astype(BF16)
    for dst, src in ((COL_DV, 768), (COL_NQ, 1024), (COL_NK, 1280), (COL_NV, 1536)):
        for b, y in enumerate(proj(src)):
            p_ref[0, :, dst + b * LANES:dst + (b + 1) * LANES] = y.astype(BF16)
    ones_bd = ones_ref[...]
    lane = _lane_iota((x.shape[0], LANES))
    low = lane < HEAD_DIM
    for b, y in enumerate(proj(1792)):
        sl = slice(b * LANES, (b + 1) * LANES)
        y = _head_rms(y, ones_bd, qn_ref[...])
        y = _rope(y, cg_ref[:, sl], sg_ref[:, sl], HEAD_DIM // 4) * (HEAD_DIM ** -0.5 * LOG2E)
        swapped = pltpu.roll(y, HEAD_DIM, 1)
        keep = low if b == 0 else jnp.logical_not(low)
        even = jnp.where(keep, y if b == 0 else swapped, 0.0)
        odd = jnp.where(keep, swapped if b == 0 else y, 0.0)
        p_ref[0, :, COL_GQ + (2 * b) * LANES:COL_GQ + (2 * b + 1) * LANES] = even.astype(BF16)
        p_ref[0, :, COL_GQ + (2 * b + 1) * LANES:COL_GQ + (2 * b + 2) * LANES] = odd.astype(BF16)
    gk, gv = proj(2048)
    y = _head_rms(gk, ones_bd, kn_ref[...])
    y = _rope(y, cg_ref[:, 0:LANES], sg_ref[:, 0:LANES], HEAD_DIM // 4)
    p_ref[0, :, COL_GK:COL_GK + LANES] = y.astype(BF16)
    p_ref[0, :, COL_GV:COL_GV + LANES] = gv.astype(BF16)


def _inproj(tokens, n_all, modv, g_mix, w_proj, tabs, ones_bd, q_norm, k_norm, n_lat):
    bsz, _, d = tokens[0].shape
    tm = TOKEN_TILE
    lat_tiles = n_lat // tm
    cd, sd, cg, sg = tabs
    full = lambda shape: pl.BlockSpec(shape, lambda b, t: (0,) * len(shape))
    tab = pl.BlockSpec((tm, 256), lambda b, t: (t, 0))
    return pl.pallas_call(
        functools.partial(_inproj_kernel, lat_tiles=lat_tiles),
        grid=(bsz, n_all // tm),
        in_specs=[*_token_specs(tokens, tm, lat_tiles),
                  pl.BlockSpec((1, 1, 6, d), lambda b, t: (b, jnp.minimum(t // lat_tiles, 1), 0, 0)),
                  full((1, d)), full((d, PROJ_COLS)), tab, tab, tab, tab,
                  full((LANES, LANES)), full((1, LANES)), full((1, LANES))],
        out_specs=[pl.BlockSpec((1, tm, BRANCH_W), lambda b, t: (b, t, 0)),
                   pl.BlockSpec((1, tm, PROJ_COLS), lambda b, t: (b, t, 0))],
        out_shape=[jax.ShapeDtypeStruct((bsz, n_all, BRANCH_W), F32),
                   jax.ShapeDtypeStruct((bsz, n_all, PROJ_COLS), BF16)],
        compiler_params=_params("parallel", "parallel"),
        name="inproj",
    )(tokens[0], tokens[1], modv, g_mix, w_proj, cd, sd, cg, sg, ones_bd, q_norm, k_norm)


POOL_HALO = 8


def _pool_specs(u, tm):
    _, n_all, ch = u.shape
    per = tm // POOL_HALO
    nblk = n_all // POOL_HALO
    return [pl.BlockSpec((1, POOL_HALO, ch), lambda b, t: (b, jnp.maximum(t * per - 1, 0), 0)),
            pl.BlockSpec((1, tm, ch), lambda b, t: (b, t, 0)),
            pl.BlockSpec((1, POOL_HALO, ch), lambda b, t: (b, jnp.minimum((t + 1) * per, nblk - 1), 0))]


def _pool_branch(prev_ref, u_ref, next_ref, w_ref, scale_ref, pad_ref, n_lat, n_all):
    t = pl.program_id(1)
    tm = TOKEN_TILE
    lat_tiles = n_lat // tm
    all_tiles = n_all // tm
    halo = POOL_HALO
    u = u_ref[0]
    first = jnp.logical_or(t == 0, t == lat_tiles)
    last = jnp.logical_or(t == lat_tiles - 1, t == all_tiles - 1)
    pad_ref[0:halo, :] = jnp.where(first, 0.0, prev_ref[0])
    pad_ref[halo:halo + tm, :] = u
    pad_ref[halo + tm:2 * halo + tm, :] = jnp.where(last, 0.0, next_ref[0])

    def shifted(d):
        return pad_ref[halo + d:halo + d + tm, :]

    sums = []
    acc = None
    for hw in POOL_HALF_WINDOWS:
        lo = hw // 2 if acc is not None else 0
        for d in range(lo, hw):
            term = shifted(d) + shifted(-d - 1)
            acc = term if acc is None else acc + term
        sums.append(acc)
    lane = _lane_iota((tm, BRANCH_W))
    group = lane // POOL_GW
    win = jnp.where(group == 0, sums[0], jnp.where(group == 1, sums[1], jnp.where(group == 2, sums[2], sums[3])))
    half = jnp.where(group == 0, 1, jnp.where(group == 1, 2, jnp.where(group == 2, 4, 8)))
    seg_start = jnp.where(t >= lat_tiles, n_lat, 0)
    seg_len = jnp.where(t >= lat_tiles, n_all - n_lat, n_lat)
    pos = t * tm - seg_start + lax.broadcasted_iota(jnp.int32, (tm, BRANCH_W), 0)
    cnt = jnp.minimum(pos + half, seg_len) - jnp.maximum(pos - half, 0)
    pooled = (win / cnt.astype(F32) - u).astype(BF16)
    y = jnp.dot(pooled, w_ref[...], preferred_element_type=F32) * scale_ref[...]
    return y.astype(BF16)


ATTN_Q_TILE = 256
ATTN_ROW_BLOCK = 64
ATTN_K_TILE = 2048


def _flash_scratch(rows, n_all, n_lat, tk, n_value_sets):
    wide = lambda: pltpu.VMEM((rows, LANES), F32)
    return [pltpu.VMEM((rows, LANES), BF16), pltpu.VMEM((2, rows, tk), F32),
            pltpu.VMEM((rows, n_all - n_lat), F32), pltpu.VMEM((rows, tk), BF16), wide(), wide(), wide(),
            pltpu.VMEM((n_value_sets, n_all, LANES), BF16)]


def _values_with_ones(v, own_half_mask):
    return jnp.where(own_half_mask, v, jnp.ones_like(v))


def _flash_attention(k_ref, scratch, value_rows, is_latent_block, n_lat, n_all):
    q_scr, s_scr, sc_scr, p_scr, m_scr, a_scr, acc_scr, v_scr = scratch
    rows = q_scr.shape[0]
    tk = p_scr.shape[1]
    rb = ATTN_ROW_BLOCK
    n_chunks = n_lat // tk
    n_ctx = n_all - n_lat
    assert n_chunks % 2 == 0 and n_chunks * tk == n_lat and n_ctx <= tk

    m_scr[...] = jnp.full(m_scr.shape, -jnp.inf, F32)
    acc_scr[...] = jnp.zeros(acc_scr.shape, F32)

    def produce(dst, start, size):
        dst[...] = _dot_nt(q_scr[...], k_ref[0, pl.ds(start, size), :])

    def consume(src, start, size):
        for r in range(rows // rb):
            rs = slice(r * rb, (r + 1) * rb)
            s = src[rs, :]
            m_old = m_scr[rs, :]
            m_new = jnp.maximum(m_old, jnp.broadcast_to(jnp.max(s, axis=-1, keepdims=True), (rb, LANES)))
            for b in range(size // LANES):
                cols = slice(b * LANES, (b + 1) * LANES)
                p_scr[rs, cols] = jnp.exp2(s[:, cols] - m_new).astype(BF16)
            m_scr[rs, :] = m_new
            a_scr[rs, :] = jnp.exp2(m_old - m_new)
        for r0, r1, vset in value_rows:
            pv = jnp.dot(p_scr[r0:r1, 0:size], v_scr[vset, pl.ds(start, size), :], preferred_element_type=F32)
            acc_scr[r0:r1, :] = a_scr[r0:r1, :] * acc_scr[r0:r1, :] + pv

    def lat(chunk):
        return pl.multiple_of(chunk * tk, tk)

    produce(sc_scr, n_lat, n_ctx)

    @pl.when(is_latent_block)
    def _():
        produce(s_scr.at[0], 0, tk)

        def step(i, carry):
            produce(s_scr.at[1], lat(2 * i + 1), tk)
            consume(s_scr.at[0], lat(2 * i), tk)
            produce(s_scr.at[0], lat(2 * i + 2), tk)
            consume(s_scr.at[1], lat(2 * i + 1), tk)
            return carry

        lax.fori_loop(0, n_chunks // 2 - 1, step, 0)
        produce(s_scr.at[1], (n_chunks - 1) * tk, tk)
        consume(s_scr.at[0], (n_chunks - 2) * tk, tk)
        consume(s_scr.at[1], (n_chunks - 1) * tk, tk)

    consume(sc_scr, n_lat, n_ctx)
    acc = acc_scr[...]
    return acc / pltpu.roll(acc, HEAD_DIM, 1)


def _diff_kernel(lam_ref, g_ref, ones_ref, q_ref, k_ref, v_ref, o_ref, *scratch, n_lat, n_all, tq, lam_init):
    i = pl.program_id(2)
    lf = lam_ref[...]
    lam = (jnp.exp(jnp.sum(lf[0:1] * lf[1:2], axis=-1, keepdims=True))
           - jnp.exp(jnp.sum(lf[2:3] * lf[3:4], axis=-1, keepdims=True)) + lam_init)
    q = q_ref[0]
    lane = _lane_iota((tq, LANES))
    q_scr, v_scr = scratch[0], scratch[-1]
    for n in range(LANES // DIFF_DQ):
        start = n * DIFF_DQ
        mine = jnp.logical_and(lane >= start, lane < start + DIFF_DQ)
        q_scr[n * tq:(n + 1) * tq, :] = jnp.where(mine, q, jnp.zeros_like(q))

    @pl.when(i == 0)
    def _():
        v = v_ref[0]
        low = _lane_iota(v.shape) < HEAD_DIM
        v_scr[0] = _values_with_ones(v, low)
        v_scr[1] = _values_with_ones(v, jnp.logical_not(low))

    out = _flash_attention(k_ref, scratch, [(0, 2 * tq, 0), (2 * tq, 4 * tq, 1)], i * tq < n_lat, n_lat, n_all)
    o = jnp.where(lane < HEAD_DIM, out[0:tq] - lam * out[tq:2 * tq], out[2 * tq:3 * tq] - lam * out[3 * tq:4 * tq])
    ms = _dot_split(o * o, ones_ref[...])
    o_ref[0] = (o * lax.rsqrt(ms + RMS_EPS) * g_ref[...] * (1.0 - lam_init)).astype(BF16)


def _diff_attention(proj, lam_p, norm_g, ones_bd, n_lat, layer, n_out):
    bsz, n_all, _ = proj.shape
    tq = ATTN_Q_TILE
    lam_init = 0.8 - 0.6 * math.exp(-0.3 * layer)
    qb, kb, vb = COL_DQ // LANES, COL_DK // LANES, COL_DV // LANES
    return pl.pallas_call(
        functools.partial(_diff_kernel, n_lat=n_lat, n_all=n_all, tq=tq, lam_init=lam_init),
        grid=(bsz, 2, n_out // tq),
        in_specs=[pl.BlockSpec(lam_p.shape, lambda b, p, i: (0, 0)),
                  pl.BlockSpec((1, LANES), lambda b, p, i: (0, 0)),
                  pl.BlockSpec((LANES, LANES), lambda b, p, i: (0, 0)),
                  pl.BlockSpec((1, tq, LANES), lambda b, p, i: (b, i, qb + p)),
                  pl.BlockSpec((1, n_all, LANES), lambda b, p, i: (b, 0, kb + p)),
                  pl.BlockSpec((1, n_all, LANES), lambda b, p, i: (b, 0, vb + p))],
        out_specs=pl.BlockSpec((1, tq, LANES), lambda b, p, i: (b, i, p)),
        out_shape=jax.ShapeDtypeStruct((bsz, n_out, BRANCH_W), BF16),
        scratch_shapes=_flash_scratch(4 * tq, n_all, n_lat, min(ATTN_K_TILE, n_lat // 2), 2),
        compiler_params=_params("parallel", "parallel", "arbitrary"),
        name="diff_attention",
    )(lam_p, norm_g, ones_bd, proj, proj, proj)


def _gqa_kernel(q_ref, k_ref, v_ref, o_ref, *scratch, n_lat, n_all, tq):
    kv = pl.program_id(1)
    i = pl.program_id(2)
    q_scr, v_scr = scratch[0], scratch[-1]
    for g in range(2):
        q_scr[g * tq:(g + 1) * tq, :] = q_ref[0, :, g * LANES:(g + 1) * LANES]

    @pl.when(i == 0)
    def _():
        v = v_ref[0]
        head_of_lane = _lane_iota(v.shape) // HEAD_DIM
        v_scr[0] = _values_with_ones(v, head_of_lane == kv)

    out = _flash_attention(k_ref, scratch, [(0, 2 * tq, 0)], i * tq < n_lat, n_lat, n_all)
    outs = [out[0:tq], out[tq:2 * tq]]
    placed = [jnp.where(kv == g, outs[g], pltpu.roll(outs[g], HEAD_DIM, 1)) for g in range(2)]
    lane = _lane_iota((tq, LANES))
    o_ref[0] = jnp.where(lane < HEAD_DIM, placed[0], placed[1]).astype(BF16)


def _gqa_attention(proj, n_lat, n_out):
    bsz, n_all, _ = proj.shape
    tq = ATTN_Q_TILE
    qb, kb, vb = COL_GQ // (2 * LANES), COL_GK // LANES, COL_GV // LANES
    return pl.pallas_call(
        functools.partial(_gqa_kernel, n_lat=n_lat, n_all=n_all, tq=tq),
        grid=(bsz, 2, n_out // tq),
        in_specs=[pl.BlockSpec((1, tq, 2 * LANES), lambda b, p, i: (b, i, qb + p)),
                  pl.BlockSpec((1, n_all, LANES), lambda b, p, i: (b, 0, kb)),
                  pl.BlockSpec((1, n_all, LANES), lambda b, p, i: (b, 0, vb))],
        out_specs=pl.BlockSpec((1, tq, LANES), lambda b, p, i: (b, i, p)),
        out_shape=jax.ShapeDtypeStruct((bsz, n_out, BRANCH_W), BF16),
        scratch_shapes=_flash_scratch(2 * tq, n_all, n_lat, min(ATTN_K_TILE, n_lat // 2), 1),
        compiler_params=_params("parallel", "parallel", "arbitrary"),
        name="gqa_attention",
    )(proj, proj, proj)


def _nat_bias(rpb):
    kc = NAT_WIN_C
    j = np.arange(GRID_W)
    col_start = np.clip(j - kc // 2, 0, GRID_W - kc)
    valid = (j[None, :] >= col_start[:, None]) & (j[None, :] < col_start[:, None] + kc)
    dc = np.clip(j[None, :] - j[:, None], -(kc - 1), kc - 1) + (kc - 1)
    tab = rpb[:, :, dc]
    tab = jnp.where(jnp.asarray(valid)[None, None], tab.astype(F32), NEG_INF)
    heads = rpb.shape[0]
    slabs = []
    for dr0 in range(NAT_WIN_R):
        blk = tab[:, dr0:dr0 + NAT_WIN_R]
        slabs.append(blk.transpose(0, 2, 1, 3).reshape(heads * GRID_W, NAT_WIN_R * GRID_W))
    return jnp.stack(slabs)


NAT_ROWS_PER_STEP = 4


def _nat_kernel(bias_ref, q_ref, k_ref, v_ref, o_ref, *, n_lat, n_all, n_out):
    w = GRID_W
    rows = n_lat // w
    heads = BRANCH_W // HEAD_DIM
    scale = HEAD_DIM ** -0.5
    lane = _lane_iota((w, BRANCH_W))
    head_of_lane = lane // HEAD_DIM
    k_ctx = k_ref[0, n_lat:n_all, :]
    v_ctx = v_ref[0, n_lat:n_all, :]

    def stacked_queries(start):
        q = q_ref[0, pl.ds(start, w), :]
        return jnp.concatenate([jnp.where(head_of_lane == h, q, jnp.zeros_like(q)) for h in range(heads)], axis=0)

    def unstack(o):
        out = jnp.zeros((w, BRANCH_W), F32)
        for h in range(heads):
            out = jnp.where(head_of_lane == h, o[h * w:(h + 1) * w, :], out)
        return out

    def grid_row(r):
        r0 = jnp.clip(r - NAT_WIN_R // 2, 0, rows - NAT_WIN_R)
        dr0 = r0 - r + (NAT_WIN_R - 1)
        qs = stacked_queries(pl.multiple_of(r * w, w))
        kb = k_ref[0, pl.ds(pl.multiple_of(r0 * w, w), NAT_WIN_R * w), :]
        vb = v_ref[0, pl.ds(pl.multiple_of(r0 * w, w), NAT_WIN_R * w), :]
        s_band = _dot_nt(qs, kb) * scale + bias_ref[dr0]
        s_ctx = _dot_nt(qs, k_ctx) * scale
        m = jnp.maximum(jnp.max(s_band, axis=-1, keepdims=True), jnp.max(s_ctx, axis=-1, keepdims=True))
        p_band = jnp.exp(s_band - m)
        p_ctx = jnp.exp(s_ctx - m)
        total = jnp.sum(p_band, axis=-1, keepdims=True) + jnp.sum(p_ctx, axis=-1, keepdims=True)
        o = (jnp.dot(p_band.astype(BF16), vb, preferred_element_type=F32)
             + jnp.dot(p_ctx.astype(BF16), v_ctx, preferred_element_type=F32)) / total
        o_ref[0, pl.ds(pl.multiple_of(r * w, w), w), :] = unstack(o).astype(BF16)

    def row_group(g, carry):
        for u in range(NAT_ROWS_PER_STEP):
            grid_row(g * NAT_ROWS_PER_STEP + u)
        return carry

    assert rows % NAT_ROWS_PER_STEP == 0
    lax.fori_loop(0, rows // NAT_ROWS_PER_STEP, row_group, 0)

    def ctx_block(cb, carry):
        start = pl.multiple_of(n_lat + cb * w, w)
        qs = stacked_queries(start)
        s = _dot_nt(qs, k_ctx) * scale
        p = jnp.exp(s - jnp.max(s, axis=-1, keepdims=True))
        o = jnp.dot(p.astype(BF16), v_ctx, preferred_element_type=F32) / jnp.sum(p, axis=-1, keepdims=True)
        o_ref[0, pl.ds(start, w), :] = unstack(o).astype(BF16)
        return carry

    lax.fori_loop(0, (n_out - n_lat) // w, ctx_block, 0)


def _nat_attention(proj, bias, n_lat, n_out):
    bsz, n_all, _ = proj.shape
    qb, kb, vb = COL_NQ // BRANCH_W, COL_NK // BRANCH_W, COL_NV // BRANCH_W
    return pl.pallas_call(
        functools.partial(_nat_kernel, n_lat=n_lat, n_all=n_all, n_out=n_out),
        grid=(bsz,),
        in_specs=[pl.BlockSpec(bias.shape, lambda b: (0, 0, 0)),
                  pl.BlockSpec((1, n_all, BRANCH_W), lambda b: (b, 0, qb)),
                  pl.BlockSpec((1, n_all, BRANCH_W), lambda b: (b, 0, kb)),
                  pl.BlockSpec((1, n_all, BRANCH_W), lambda b: (b, 0, vb))],
        out_specs=pl.BlockSpec((1, n_out, BRANCH_W), lambda b: (b, 0, 0)),
        out_shape=jax.ShapeDtypeStruct((bsz, n_out, BRANCH_W), BF16),
        compiler_params=_params("parallel"),
        name="nat_attention",
    )(bias, proj, proj, proj)


def _route(logits):
    lane = _lane_iota(logits.shape)
    lane_f = lane.astype(F32)
    far = float(LANES)
    gl = jnp.where(lane < N_GROUPS, logits, -jnp.inf)
    gmax = jnp.max(gl, axis=-1, keepdims=True)
    grp = jnp.min(jnp.where(gl == gmax, lane_f, far), axis=-1, keepdims=True)
    w_grp = 1.0 / jnp.sum(jnp.exp(gl - gmax), axis=-1, keepdims=True)
    first = N_GROUPS + EXPERTS_PER_GROUP * grp
    el = jnp.where(jnp.logical_and(lane_f >= first, lane_f < first + EXPERTS_PER_GROUP), logits, -jnp.inf)
    v1 = jnp.max(el, axis=-1, keepdims=True)
    i1 = jnp.min(jnp.where(el == v1, lane_f, far), axis=-1, keepdims=True)
    el2 = jnp.where(lane_f == i1, -jnp.inf, el)
    v2 = jnp.max(el2, axis=-1, keepdims=True)
    i2 = jnp.min(jnp.where(el2 == v2, lane_f, far), axis=-1, keepdims=True)
    t = jnp.exp(v2 - v1)
    w1 = w_grp / (1.0 + t)
    w2 = w_grp * t / (1.0 + t)
    ids = jnp.where(lane == 0, i1 - N_GROUPS, jnp.where(lane == 1, i2 - N_GROUPS, 0.0)).astype(jnp.int32)
    wts = jnp.where(lane == 0, w1, jnp.where(lane == 1, w2, 0.0))
    return ids, wts


def _merge_kernel(xa_ref, xb_ref, mod_ref, gm_ref, gf_ref, up_ref, u_ref, un_ref, pw_ref, ps_ref,
                  b1_ref, b2_ref, b3_ref, wg_ref, wb_ref, wo_ref, wrh_ref, wrl_ref, br_ref,
                  x1_ref, h2_ref, ids_ref, wts_ref, pad_ref, *, n_lat, n_all):
    d = xa_ref.shape[-1]
    lat_tiles = n_lat // TOKEN_TILE
    x = _token_tile(xa_ref, xb_ref, lat_tiles)
    mod = lambda k: mod_ref[0, 0, k:k + 1, :]
    h = _modulated_norm(x, gm_ref[...], mod(0), mod(1)).astype(BF16)
    pooled = _pool_branch(up_ref, u_ref, un_ref, pw_ref, ps_ref, pad_ref, n_lat, n_all)
    merged = None
    for i, branch in enumerate((pooled, b1_ref[0], b2_ref[0], b3_ref[0])):
        gate = jax.nn.sigmoid(jnp.dot(h, wg_ref[:, i * d:(i + 1) * d], preferred_element_type=F32))
        term = gate * jnp.dot(branch, wb_ref[i], preferred_element_type=F32)
        merged = term if merged is None else merged + term
    o = jnp.dot(merged.astype(BF16), wo_ref[...], preferred_element_type=F32)
    x1 = x + mod(2) * o
    x1_ref[0] = x1
    h2 = _modulated_norm(x1, gf_ref[...], mod(3), mod(4))
    h2_ref[0] = _pack_halves(h2)
    logits = _dot_split(h2, wrh_ref[...], wrl_ref[...]) + br_ref[...]
    ids, wts = _route(logits)
    ids_ref[0] = ids
    wts_ref[0] = wts


def _merge(tokens, n_out, modv, g_mix, g_ffn, u, pool_w, pool_scale, branches, w_gate, w_branch, w_out, w_router,
           b_router, n_lat):
    bsz, _, d = tokens[0].shape
    n_all = u.shape[1]
    tm = TOKEN_TILE
    lat_tiles = n_lat // tm
    tile = lambda width: pl.BlockSpec((1, tm, width), lambda b, t: (b, t, 0))
    full = lambda shape: pl.BlockSpec(shape, lambda b, t: (0,) * len(shape))
    return pl.pallas_call(
        functools.partial(_merge_kernel, n_lat=n_lat, n_all=n_all),
        grid=(bsz, n_out // tm),
        in_specs=[*_token_specs(tokens, tm, lat_tiles),
                  pl.BlockSpec((1, 1, 6, d), lambda b, t: (b, jnp.minimum(t // lat_tiles, 1), 0, 0)),
                  full((1, d)), full((1, d)),
                  *_pool_specs(u, tm), full(pool_w.shape), full(pool_scale.shape),
                  tile(BRANCH_W), tile(BRANCH_W), tile(BRANCH_W),
                  full(w_gate.shape), full(w_branch.shape), full(w_out.shape),
                  full(w_router.shape), full(w_router.shape), full(b_router.shape)],
        out_specs=[tile(d), tile(d // 2), tile(LANES), tile(LANES)],
        out_shape=[jax.ShapeDtypeStruct((bsz, n_out, d), F32), jax.ShapeDtypeStruct((bsz, n_out, d // 2), jnp.int32),
                   jax.ShapeDtypeStruct((bsz, n_out, LANES), jnp.int32),
                   jax.ShapeDtypeStruct((bsz, n_out, LANES), F32)],
        scratch_shapes=[pltpu.VMEM((tm + 2 * POOL_HALO, BRANCH_W), F32)],
        compiler_params=_params("parallel", "parallel"),
        name="merge_router",
    )(tokens[0], tokens[1], modv, g_mix, g_ffn, u, u, u, pool_w, pool_scale, *branches, w_gate, w_branch, w_out,
      *_split_bf16(w_router), b_router)


def _slot_plan(ids):
    te = EXPERT_TILE
    flat = ids.reshape(-1)
    n_pairs = flat.shape[0]
    n_tiles = n_pairs // te + N_EXPERTS
    onehot = (flat[:, None] == jnp.arange(N_EXPERTS, dtype=jnp.int32)[None, :]).astype(jnp.int32)
    running = jnp.cumsum(onehot, axis=0)
    counts = running[-1]
    rank = jnp.sum(running * onehot, axis=1) - 1
    padded = ((counts + te - 1) // te) * te
    ends = jnp.cumsum(padded)
    offsets = ends - padded
    dest = (jnp.sum(onehot * offsets[None, :], axis=1) + rank).astype(jnp.int32)
    tile_start = jnp.arange(n_tiles, dtype=jnp.int32) * te
    tile_expert = jnp.minimum(jnp.sum((tile_start[:, None] >= ends[None, :]).astype(jnp.int32), axis=1),
                              N_EXPERTS - 1).astype(jnp.int32)
    n_used = (ends[-1] // te).astype(jnp.int32).reshape(1)
    return dest, tile_expert, n_used, n_tiles


SUBLANES = 8
ROW_DMA_UNROLL = SUBLANES


def _hbm_row(ref, row):
    return ref.at[pl.ds(row, 1), :]


def _vmem_row(ref, group, sub):
    return ref.at[group, pl.ds(sub, 1), :]


def _wait_rows(make_copy, n_rows):
    def wait(r, carry):
        for _ in range(ROW_DMA_UNROLL):
            make_copy().wait()
        return carry

    lax.fori_loop(0, n_rows // ROW_DMA_UNROLL, wait, 0)


def _dispatch_kernel(dest_ref, h_ref, init_ref, out_ref, stage, stage_sems, row_sem, *, tm):
    del init_ref
    i = pl.program_id(0)
    n_steps = pl.num_programs(0)
    slot = i % 2
    groups = tm // SUBLANES

    def stage_copy(tile, into):
        return pltpu.make_async_copy(h_ref.at[pl.ds(tile * groups, groups)], stage.at[into], stage_sems.at[into])

    def row_copy(group, sub, dst_row):
        return pltpu.make_async_copy(_vmem_row(stage.at[slot], group, sub), _hbm_row(out_ref, dst_row), row_sem)

    wait_all = functools.partial(_wait_rows, lambda: row_copy(0, 0, 0), 2 * tm)

    @pl.when(i == 0)
    def _():
        stage_copy(0, 0).start()

    pl.when(i > 0)(wait_all)

    @pl.when(i + 1 < n_steps)
    def _():
        stage_copy(i + 1, 1 - slot).start()

    stage_copy(i, slot).wait()

    def start(g, carry):
        for u in range(SUBLANES):
            for k in range(2):
                row_copy(g, u, dest_ref[0, 0, 2 * (g * SUBLANES + u) + k]).start()
        return carry

    lax.fori_loop(0, groups, start, 0)
    pl.when(i == n_steps - 1)(wait_all)


def _dispatch(dest, h2, n_slots):
    n_tok, d = h2.shape
    tm = TOKEN_TILE
    groups = tm // SUBLANES
    return pl.pallas_call(
        functools.partial(_dispatch_kernel, tm=tm),
        grid=(n_tok // tm,),
        in_specs=[pl.BlockSpec((1, 1, 2 * tm), lambda i: (i, 0, 0), memory_space=pltpu.SMEM),
                  pl.BlockSpec(memory_space=pl.ANY),
                  pl.BlockSpec(memory_space=pl.ANY)],
        out_specs=pl.BlockSpec(memory_space=pl.ANY),
        out_shape=jax.ShapeDtypeStruct((n_slots, d), h2.dtype),
        scratch_shapes=[pltpu.VMEM((2, groups, SUBLANES, d), h2.dtype), pltpu.SemaphoreType.DMA((2,)),
                        pltpu.SemaphoreType.DMA(())],
        input_output_aliases={2: 0},
        compiler_params=_params("arbitrary"),
        name="moe_dispatch",
    )(dest.reshape(n_tok // tm, 1, 2 * tm), h2.reshape(n_tok // SUBLANES, SUBLANES, d),
      jnp.zeros((n_slots, d), h2.dtype))


def _ffn_kernel(te_ref, used_ref, x_ref, wg_ref, wu_ref, wd_ref, o_ref, wg_scr, wu_scr, wd_scr):
    i = pl.program_id(0)
    used = i < used_ref[0]
    new_expert = jnp.logical_or(i == 0, te_ref[i] != te_ref[jnp.maximum(i - 1, 0)])

    @pl.when(jnp.logical_and(used, new_expert))
    def _():
        wg_scr[...] = wg_ref[0].astype(BF16)
        wu_scr[...] = wu_ref[0].astype(BF16)
        wd_scr[...] = wd_ref[0].astype(BF16)

    @pl.when(used)
    def _():
        hi, lo = _unpack_halves(x_ref[...])
        x = jnp.concatenate([hi.astype(BF16), lo.astype(BF16)], axis=1)
        g = jnp.dot(x, wg_scr[...], preferred_element_type=F32)
        u = jnp.dot(x, wu_scr[...], preferred_element_type=F32)
        a = (g * jax.nn.sigmoid(g) * u).astype(BF16)
        o_ref[...] = _pack_halves(jnp.dot(a, wd_scr[...], preferred_element_type=F32))

    @pl.when(jnp.logical_not(used))
    def _():
        o_ref[...] = jnp.zeros_like(o_ref)


def _expert_ffn(tile_expert, n_used, xs, w_gate, w_up, w_down, n_tiles, layer):
    n_slots, half = xs.shape
    d = 2 * half
    te = EXPERT_TILE
    de = w_gate.shape[-1]
    grid_spec = pltpu.PrefetchScalarGridSpec(
        num_scalar_prefetch=2,
        grid=(n_tiles,),
        in_specs=[pl.BlockSpec((te, half), lambda i, te_ref, used: (i, 0)),
                  pl.BlockSpec((None, 1, d, de), lambda i, te_ref, used: (layer, te_ref[i], 0, 0)),
                  pl.BlockSpec((None, 1, d, de), lambda i, te_ref, used: (layer, te_ref[i], 0, 0)),
                  pl.BlockSpec((None, 1, de, d), lambda i, te_ref, used: (layer, te_ref[i], 0, 0))],
        out_specs=pl.BlockSpec((te, half), lambda i, te_ref, used: (i, 0)),
        scratch_shapes=[pltpu.VMEM((d, de), BF16), pltpu.VMEM((d, de), BF16), pltpu.VMEM((de, d), BF16)],
    )
    return pl.pallas_call(
        _ffn_kernel,
        grid_spec=grid_spec,
        out_shape=jax.ShapeDtypeStruct((n_slots, half), jnp.int32),
        compiler_params=_params("arbitrary"),
        name="moe_ffn",
    )(tile_expert, n_used, xs, w_gate, w_up, w_down)


def _combine_kernel(dest_ref, next_ref, wts_ref, x1_ref, mod_ref, gf_ref, ys_ref, o_ref, buf, sems, *, final):
    tm = x1_ref.shape[1]
    step = pl.program_id(0) * pl.num_programs(1) + pl.program_id(1)
    n_steps = pl.num_programs(0) * pl.num_programs(1)
    slot = step % 2

    def row_copy(src_row, into, k, group, sub):
        return pltpu.make_async_copy(_hbm_row(ys_ref, src_row), _vmem_row(buf.at[into, k], group, sub),
                                     sems.at[into])

    def gather(idx_ref, into):
        def start(g, carry):
            for u in range(SUBLANES):
                for k in range(2):
                    row_copy(idx_ref[0, 0, 2 * (g * SUBLANES + u) + k], into, k, g, u).start()
            return carry

        lax.fori_loop(0, tm // SUBLANES, start, 0)

    @pl.when(step == 0)
    def _():
        gather(dest_ref, 0)

    @pl.when(step + 1 < n_steps)
    def _():
        gather(next_ref, 1 - slot)

    _wait_rows(lambda: row_copy(0, slot, 0, 0, 0), 2 * tm)
    wts = wts_ref[0]
    parts = []
    for k in range(2):
        hi, lo = _unpack_halves(buf[slot, k].reshape(tm, buf.shape[-1]))
        parts.append(wts[:, k:k + 1] * jnp.concatenate([hi, lo], axis=1))
    x2 = x1_ref[0] + mod_ref[0, 0, 5:6, :] * (parts[0] + parts[1])
    if final:
        ms = jnp.mean(x2 * x2, axis=-1, keepdims=True)
        x2 = x2 * lax.rsqrt(ms + RMS_EPS) * gf_ref[...]
    o_ref[0] = x2


def _combine(dest, wts, x1, modv, g_final, ys, n_lat, final):
    bsz, n_all, d = x1.shape
    tm = TOKEN_TILE
    lat_tiles = n_lat // tm
    all_tiles = n_all // tm
    tiles = lat_tiles if final else all_tiles
    n_out = n_lat if final else n_all

    def dest_row(step):
        step = jnp.minimum(step, bsz * tiles - 1)
        return (step // tiles) * all_tiles + step % tiles

    idx_spec = lambda ahead: pl.BlockSpec((1, 1, 2 * tm), lambda b, t: (dest_row(b * tiles + t + ahead), 0, 0),
                                          memory_space=pltpu.SMEM)
    dest3 = dest.reshape(bsz * all_tiles, 1, 2 * tm)
    return pl.pallas_call(
        functools.partial(_combine_kernel, final=final),
        grid=(bsz, tiles),
        in_specs=[idx_spec(0), idx_spec(1),
                  pl.BlockSpec((1, tm, LANES), lambda b, t: (b, t, 0)),
                  pl.BlockSpec((1, tm, d), lambda b, t: (b, t, 0)),
                  pl.BlockSpec((1, 1, 6, d), lambda b, t: (b, jnp.minimum(t // lat_tiles, 1), 0, 0)),
                  pl.BlockSpec((1, d), lambda b, t: (0, 0)),
                  pl.BlockSpec(memory_space=pl.ANY)],
        out_specs=pl.BlockSpec((1, tm, d), lambda b, t: (b, t, 0)),
        out_shape=jax.ShapeDtypeStruct((bsz, n_out, d), F32),
        scratch_shapes=[pltpu.VMEM((2, 2, tm // SUBLANES, SUBLANES, d // 2), jnp.int32),
                        pltpu.SemaphoreType.DMA((2,))],
        compiler_params=_params("arbitrary", "arbitrary"),
        name="moe_combine",
    )(dest3, dest3, wts, x1, modv, g_final, ys)


def kernel(x, c, ctx, c_ctx, w_mod, b_mod, g_mix, g_ffn, w_in, pool_w, pool_scale, diff_lambda, diff_norm_g, nat_rpb, gqa_q_norm, gqa_k_norm, w_branch, w_out, w_router_group, b_router_group, w_router_expert, b_router_expert, w_exp_gate, w_exp_up, w_exp_down, g_final):
    bsz, n_lat, d = x.shape
    n_ctx = ctx.shape[1]
    n_all = n_lat + n_ctx
    depth = w_mod.shape[0]
    tm = TOKEN_TILE
    assert n_lat % tm == 0 and n_ctx % tm == 0 and n_lat % GRID_W == 0 and n_lat // GRID_W >= NAT_WIN_R
    assert d == w_in.shape[1] and w_in.shape[2] == PROJ_COLS + N_BRANCH * d

    tokens = (x, ctx, 0)
    mod_rows = 16
    assert bsz + 1 <= mod_rows
    cc = jnp.zeros((mod_rows, d), F32).at[:bsz].set(c).at[bsz].set(c_ctx)
    mod = _modulation(cc, w_mod, b_mod)
    mod_lat = mod[:, :bsz].reshape(depth, bsz, 1, 6, d)
    mod_ctx = jnp.broadcast_to(mod[:, bsz].reshape(depth, 1, 1, 6, d), (depth, bsz, 1, 6, d))
    modv = jnp.concatenate([mod_lat, mod_ctx], axis=2)

    tabs = _rope_tables(n_lat, n_ctx, 256, DIFF_DQ) + _rope_tables(n_lat, n_ctx, 256, HEAD_DIM)
    head_of = np.arange(LANES) // HEAD_DIM
    ones_bd = jnp.asarray((head_of[:, None] == head_of[None, :]).astype(np.float32) / HEAD_DIM, BF16)
    group_of = np.arange(BRANCH_W) // POOL_GW
    pool_mask = jnp.asarray(group_of[:, None] == group_of[None, :])

    out = None
    for l in range(depth):
        final = l == depth - 1
        n_out = n_lat if final else n_all
        w_proj = w_in[l, :, :PROJ_COLS].astype(BF16)
        w_gate = w_in[l, :, PROJ_COLS:].astype(BF16)
        tile2 = lambda v: jnp.tile(v.reshape(1, HEAD_DIM), (1, LANES // HEAD_DIM))
        u, proj = _inproj(tokens, n_all, modv[l], g_mix[l].reshape(1, d), w_proj, tabs, ones_bd,
                          tile2(gqa_q_norm[l]), tile2(gqa_k_norm[l]), n_lat)
        pool_bd = jnp.where(pool_mask, jnp.tile(pool_w[l].reshape(BRANCH_W, POOL_GW), (1, BRANCH_W // POOL_GW)),
                            0.0).astype(BF16)
        b_diff = _diff_attention(proj, diff_lambda[l], tile2(diff_norm_g[l]), ones_bd, n_lat, l, n_out)
        b_nat = _nat_attention(proj, _nat_bias(nat_rpb[l]), n_lat, n_out)
        b_gqa = _gqa_attention(proj, n_lat, n_out)
        n_route = N_GROUPS + N_EXPERTS
        w_router = jnp.zeros((d, LANES), F32).at[:, :N_GROUPS].set(w_router_group[l])
        w_router = w_router.at[:, N_GROUPS:n_route].set(w_router_expert[l])
        b_router = jnp.zeros((1, LANES), F32).at[0, :N_GROUPS].set(b_router_group[l])
        b_router = b_router.at[0, N_GROUPS:n_route].set(b_router_expert[l])
        x1, h2, ids, wts = _merge(tokens, n_out, modv[l], g_mix[l].reshape(1, d), g_ffn[l].reshape(1, d),
                                  u, pool_bd, pool_scale[l].reshape(1, BRANCH_W),
                                  (b_diff, b_nat, b_gqa), w_gate, w_branch[l].astype(BF16),
                                  w_out[l].astype(BF16), w_router, b_router, n_lat)
        dest, tile_expert, n_used, n_tiles = _slot_plan(ids[:, :, :2])
        xs = _dispatch(dest, h2.reshape(bsz * n_out, d // 2), n_tiles * EXPERT_TILE)
        ys = _expert_ffn(tile_expert, n_used, xs, w_exp_gate, w_exp_up, w_exp_down, n_tiles, l)
        out = _combine(dest, wts, x1, modv[l], g_final.reshape(1, d), ys, n_lat, final)
        tokens = (out, out, n_lat // tm)
    return out
```

```python
import functools
import math

import numpy as np
import jax
import jax.numpy as jnp
from jax import lax
from jax.experimental import pallas as pl
from jax.experimental.pallas import tpu as pltpu

GRID_W = 64
HEAD_DIM = 64
BRANCH_W = 256
N_BRANCH = 4
POOL_HALF_WINDOWS = (1, 2, 4, 8)
POOL_GW = 64
DIFF_DQ = 32
NAT_WIN_R = 8
NAT_WIN_C = 16
N_GROUPS = 4
EXPERTS_PER_GROUP = 8
N_EXPERTS = 32
D_EXPERT = 512
ROPE_THETA = 10000.0
RMS_EPS = 1e-6
NEG_INF = -1e30
LOG2E = 1.4426950408889634

LANES = 128
TOKEN_TILE = 256
EXPERT_TILE = 512
VMEM_LIMIT_BYTES = 56 * 1024 * 1024

PROJ_COLS = 2304
COL_DQ, COL_DK, COL_DV = 0, 256, 512
COL_NQ, COL_NK, COL_NV = 768, 1024, 1280
COL_GQ, COL_GK, COL_GV = 1536, 2048, 2176

F32 = jnp.float32
BF16 = jnp.bfloat16
HIGHEST = lax.Precision.HIGHEST


def _params(*sem):
    return pltpu.CompilerParams(dimension_semantics=sem, vmem_limit_bytes=VMEM_LIMIT_BYTES)


def _lane_iota(shape):
    return lax.broadcasted_iota(jnp.int32, shape, len(shape) - 1)


def _dot_nt(a, b):
    return lax.dot_general(a, b, (((1,), (1,)), ((), ())), preferred_element_type=F32)


def _split_bf16(x):
    hi = x.astype(BF16)
    return hi, (x - hi.astype(F32)).astype(BF16)


def _dot_split(a, b_hi, b_lo=None):
    a_hi, a_lo = _split_bf16(a)
    out = jnp.dot(a_hi, b_hi, preferred_element_type=F32) + jnp.dot(a_lo, b_hi, preferred_element_type=F32)
    if b_lo is not None:
        out = out + jnp.dot(a_hi, b_lo, preferred_element_type=F32)
    return out


def _pack_halves(y):
    w = y.shape[1] // 2
    hi = pltpu.bitcast(y[:, :w].astype(BF16).astype(F32), jnp.int32)
    lo = pltpu.bitcast(y[:, w:].astype(BF16).astype(F32), jnp.int32)
    return jnp.bitwise_or(hi, lax.shift_right_logical(lo, 16))


def _unpack_halves(word):
    hi = pltpu.bitcast(jnp.bitwise_and(word, -65536), F32)
    lo = pltpu.bitcast(lax.shift_left(word, 16), F32)
    return hi, lo


def _modulated_norm(x, gain, shift, scale):
    ms = jnp.mean(x * x, axis=-1, keepdims=True)
    return (x * lax.rsqrt(ms + RMS_EPS)) * gain * (1.0 + scale) + shift


def _mod_kernel(c_ref, w_ref, b_ref, o_ref):
    c = c_ref[...]
    s = c * jax.nn.sigmoid(c)
    o_ref[0] = jnp.dot(s, w_ref[0], preferred_element_type=F32, precision=HIGHEST) + b_ref[0]


def _modulation(cc, w_mod, b_mod):
    depth, d, n6 = w_mod.shape
    rows = cc.shape[0]
    tn = 512
    return pl.pallas_call(
        _mod_kernel,
        grid=(depth, n6 // tn),
        in_specs=[pl.BlockSpec((rows, d), lambda l, j: (0, 0)),
                  pl.BlockSpec((1, d, tn), lambda l, j: (l, 0, j)),
                  pl.BlockSpec((1, 1, tn), lambda l, j: (l, 0, j))],
        out_specs=pl.BlockSpec((1, rows, tn), lambda l, j: (l, 0, j)),
        out_shape=jax.ShapeDtypeStruct((depth, rows, n6), F32),
        compiler_params=_params("arbitrary", "arbitrary"),
        name="modulation",
    )(cc, w_mod, b_mod.reshape(depth, 1, n6))


def _rope_tables(n_lat, n_ctx, width, vec_dim):
    t = np.arange(n_lat)
    row, col = t // GRID_W, t % GRID_W
    j = np.arange(width) % vec_dim
    half = vec_dim // 2
    quarter = half // 2
    jj = j % half
    inv = ROPE_THETA ** (-(jj % quarter).astype(np.float64) / quarter)
    pos = np.where((j < half)[None, :], row[:, None], col[:, None]).astype(np.float64)
    ang = pos * inv[None, :]
    sign = np.where(jj < quarter, -1.0, 1.0)[None, :]
    cos = np.concatenate([np.cos(ang), np.ones((n_ctx, width))], axis=0)
    sin = np.concatenate([np.sin(ang) * sign, np.zeros((n_ctx, width))], axis=0)
    return jnp.asarray(cos, F32), jnp.asarray(sin, F32)


def _rope(x, cos, sin, quarter):
    lane = _lane_iota(x.shape)
    first = (lane % (2 * quarter)) < quarter
    up = pltpu.roll(x, LANES - quarter, 1)
    down = pltpu.roll(x, quarter, 1)
    return x * cos + jnp.where(first, up, down) * sin


def _head_rms(x, ones_bd, gain):
    ms = _dot_split(x * x, ones_bd)
    return x * lax.rsqrt(ms + RMS_EPS) * gain


def _token_specs(tokens, tm, lat_tiles):
    lat, ctx, ctx_first = tokens
    d = lat.shape[-1]
    return [pl.BlockSpec((1, tm, d), lambda b, t: (b, jnp.minimum(t, lat_tiles - 1), 0)),
            pl.BlockSpec((1, tm, d), lambda b, t: (b, ctx_first + jnp.maximum(t - lat_tiles, 0), 0))]


def _token_tile(xa_ref, xb_ref, lat_tiles):
    return jnp.where(pl.program_id(1) >= lat_tiles, xb_ref[0], xa_ref[0])


def _inproj_kernel(xa_ref, xb_ref, mod_ref, g_ref, w_ref, cd_ref, sd_ref, cg_ref, sg_ref, ones_ref, qn_ref, kn_ref,
                   u_ref, p_ref, *, lat_tiles):
    x = _token_tile(xa_ref, xb_ref, lat_tiles)
    h = _modulated_norm(x, g_ref[...], mod_ref[0, 0, 0:1, :], mod_ref[0, 0, 1:2, :]).astype(BF16)

    def proj(col):
        y = jnp.dot(h, w_ref[:, col:col + 2 * LANES], preferred_element_type=F32)
        return y[:, 0:LANES], y[:, LANES:2 * LANES]

    for b, y in enumerate(proj(0)):
        u_ref[0, :, b * LANES:(b + 1) * LANES] = y
    for dst, src, mult in ((COL_DQ, 256, DIFF_DQ ** -0.5 * LOG2E), (COL_DK, 512, 1.0)):
        for b, y in enumerate(proj(src)):
            sl = slice(b * LANES, (b + 1) * LANES)
            y = _rope(y, cd_ref[:, sl], sd_ref[:, sl], DIFF_DQ // 4) * mult
            p_ref[0, :, dst + b * LANES:dst + (b + 1) * LANES] = y.astype(BF16)
    for dst, src in ((COL_DV, 768), (COL_NQ, 1024), (COL_NK, 1280), (COL_NV, 1536)):
        for b, y in enumerate(proj(src)):
            p_ref[0, :, dst + b * LANES:dst + (b + 1) * LANES] = y.astype(BF16)
    ones_bd = ones_ref[...]
    lane = _lane_iota((x.shape[0], LANES))
    low = lane < HEAD_DIM
    for b, y in enumerate(proj(1792)):
        sl = slice(b * LANES, (b + 1) * LANES)
        y = _head_rms(y, ones_bd, qn_ref[...])
        y = _rope(y, cg_ref[:, sl], sg_ref[:, sl], HEAD_DIM // 4) * (HEAD_DIM ** -0.5 * LOG2E)
        swapped = pltpu.roll(y, HEAD_DIM, 1)
        keep = low if b == 0 else jnp.logical_not(low)
        even = jnp.where(keep, y if b == 0 else swapped, 0.0)
        odd = jnp.where(keep, swapped if b == 0 else y, 0.0)
        p_ref[0, :, COL_GQ + (2 * b) * LANES:COL_GQ + (2 * b + 1) * LANES] = even.astype(BF16)
        p_ref[0, :, COL_GQ + (2 * b + 1) * LANES:COL_GQ + (2 * b + 2) * LANES] = odd.astype(BF16)
    gk, gv = proj(2048)
    y = _head_rms(gk, ones_bd, kn_ref[...])
    y = _rope(y, cg_ref[:, 0:LANES], sg_ref[:, 0:LANES], HEAD_DIM // 4)
    p_ref[0, :, COL_GK:COL_GK + LANES] = y.astype(BF16)
    p_ref[0, :, COL_GV:COL_GV + LANES] = gv.astype(BF16)


def _inproj(tokens, n_all, modv, g_mix, w_proj, tabs, ones_bd, q_norm, k_norm, n_lat):
    bsz, _, d = tokens[0].shape
    tm = TOKEN_TILE
    lat_tiles = n_lat // tm
    cd, sd, cg, sg = tabs
    full = lambda shape: pl.BlockSpec(shape, lambda b, t: (0,) * len(shape))
    tab = pl.BlockSpec((tm, 256), lambda b, t: (t, 0))
    return pl.pallas_call(
        functools.partial(_inproj_kernel, lat_tiles=lat_tiles),
        grid=(bsz, n_all // tm),
        in_specs=[*_token_specs(tokens, tm, lat_tiles),
                  pl.BlockSpec((1, 1, 6, d), lambda b, t: (b, jnp.minimum(t // lat_tiles, 1), 0, 0)),
                  full((1, d)), full((d, PROJ_COLS)), tab, tab, tab, tab,
                  full((LANES, LANES)), full((1, LANES)), full((1, LANES))],
        out_specs=[pl.BlockSpec((1, tm, BRANCH_W), lambda b, t: (b, t, 0)),
                   pl.BlockSpec((1, tm, PROJ_COLS), lambda b, t: (b, t, 0))],
        out_shape=[jax.ShapeDtypeStruct((bsz, n_all, BRANCH_W), F32),
                   jax.ShapeDtypeStruct((bsz, n_all, PROJ_COLS), BF16)],
        compiler_params=_params("parallel", "parallel"),
        name="inproj",
    )(tokens[0], tokens[1], modv, g_mix, w_proj, cd, sd, cg, sg, ones_bd, q_norm, k_norm)


POOL_HALO = 8


def _pool_specs(u, tm):
    _, n_all, ch = u.shape
    per = tm // POOL_HALO
    nblk = n_all // POOL_HALO
    return [pl.BlockSpec((1, POOL_HALO, ch), lambda b, t: (b, jnp.maximum(t * per - 1, 0), 0)),
            pl.BlockSpec((1, tm, ch), lambda b, t: (b, t, 0)),
            pl.BlockSpec((1, POOL_HALO, ch), lambda b, t: (b, jnp.minimum((t + 1) * per, nblk - 1), 0))]


def _pool_branch(prev_ref, u_ref, next_ref, w_ref, scale_ref, pad_ref, n_lat, n_all):
    t = pl.program_id(1)
    tm = TOKEN_TILE
    lat_tiles = n_lat // tm
    all_tiles = n_all // tm
    halo = POOL_HALO
    u = u_ref[0]
    first = jnp.logical_or(t == 0, t == lat_tiles)
    last = jnp.logical_or(t == lat_tiles - 1, t == all_tiles - 1)
    pad_ref[0:halo, :] = jnp.where(first, 0.0, prev_ref[0])
    pad_ref[halo:halo + tm, :] = u
    pad_ref[halo + tm:2 * halo + tm, :] = jnp.where(last, 0.0, next_ref[0])

    def shifted(d):
        return pad_ref[halo + d:halo + d + tm, :]

    sums = []
    acc = None
    for hw in POOL_HALF_WINDOWS:
        lo = hw // 2 if acc is not None else 0
        for d in range(lo, hw):
            term = shifted(d) + shifted(-d - 1)
            acc = term if acc is None else acc + term
        sums.append(acc)
    lane = _lane_iota((tm, BRANCH_W))
    group = lane // POOL_GW
    win = jnp.where(group == 0, sums[0], jnp.where(group == 1, sums[1], jnp.where(group == 2, sums[2], sums[3])))
    half = jnp.where(group == 0, 1, jnp.where(group == 1, 2, jnp.where(group == 2, 4, 8)))
    seg_start = jnp.where(t >= lat_tiles, n_lat, 0)
    seg_len = jnp.where(t >= lat_tiles, n_all - n_lat, n_lat)
    pos = t * tm - seg_start + lax.broadcasted_iota(jnp.int32, (tm, BRANCH_W), 0)
    cnt = jnp.minimum(pos + half, seg_len) - jnp.maximum(pos - half, 0)
    pooled = (win / cnt.astype(F32) - u).astype(BF16)
    y = jnp.dot(pooled, w_ref[...], preferred_element_type=F32) * scale_ref[...]
    return y.astype(BF16)


ATTN_Q_TILE = 256
ATTN_ROW_BLOCK = 64
ATTN_K_TILE = 2048


def _flash_scratch(rows, n_all, n_lat, tk, n_value_sets):
    wide = lambda: pltpu.VMEM((rows, LANES), F32)
    return [pltpu.VMEM((rows, LANES), BF16), pltpu.VMEM((2, rows, tk), F32),
            pltpu.VMEM((rows, n_all - n_lat), F32), pltpu.VMEM((rows, tk), BF16), wide(), wide(), wide(),
            pltpu.VMEM((n_value_sets, n_all, LANES), BF16)]


def _values_with_ones(v, own_half_mask):
    return jnp.where(own_half_mask, v, jnp.ones_like(v))


def _flash_attention(k_ref, scratch, value_rows, is_latent_block, n_lat, n_all):
    q_scr, s_scr, sc_scr, p_scr, m_scr, a_scr, acc_scr, v_scr = scratch
    rows = q_scr.shape[0]
    tk = p_scr.shape[1]
    rb = ATTN_ROW_BLOCK
    n_chunks = n_lat // tk
    n_ctx = n_all - n_lat
    assert n_chunks % 2 == 0 and n_chunks * tk == n_lat and n_ctx <= tk

    m_scr[...] = jnp.full(m_scr.shape, -jnp.inf, F32)
    acc_scr[...] = jnp.zeros(acc_scr.shape, F32)

    def produce(dst, start, size):
        dst[...] = _dot_nt(q_scr[...], k_ref[0, pl.ds(start, size), :])

    def consume(src, start, size):
        for r in range(rows // rb):
            rs = slice(r * rb, (r + 1) * rb)
            s = src[rs, :]
            m_old = m_scr[rs, :]
            m_new = jnp.maximum(m_old, jnp.broadcast_to(jnp.max(s, axis=-1, keepdims=True), (rb, LANES)))
            for b in range(size // LANES):
                cols = slice(b * LANES, (b + 1) * LANES)
                p_scr[rs, cols] = jnp.exp2(s[:, cols] - m_new).astype(BF16)
            m_scr[rs, :] = m_new
            a_scr[rs, :] = jnp.exp2(m_old - m_new)
        for r0, r1, vset in value_rows:
            pv = jnp.dot(p_scr[r0:r1, 0:size], v_scr[vset, pl.ds(start, size), :], preferred_element_type=F32)
            acc_scr[r0:r1, :] = a_scr[r0:r1, :] * acc_scr[r0:r1, :] + pv

    def lat(chunk):
        return pl.multiple_of(chunk * tk, tk)

    produce(sc_scr, n_lat, n_ctx)

    @pl.when(is_latent_block)
    def _():
        produce(s_scr.at[0], 0, tk)

        def step(i, carry):
            produce(s_scr.at[1], lat(2 * i + 1), tk)
            consume(s_scr.at[0], lat(2 * i), tk)
            produce(s_scr.at[0], lat(2 * i + 2), tk)
            consume(s_scr.at[1], lat(2 * i + 1), tk)
            return carry

        lax.fori_loop(0, n_chunks // 2 - 1, step, 0)
        produce(s_scr.at[1], (n_chunks - 1) * tk, tk)
        consume(s_scr.at[0], (n_chunks - 2) * tk, tk)
        consume(s_scr.at[1], (n_chunks - 1) * tk, tk)

    consume(sc_scr, n_lat, n_ctx)
    acc = acc_scr[...]
    return acc / pltpu.roll(acc, HEAD_DIM, 1)


def _diff_kernel(lam_ref, g_ref, ones_ref, q_ref, k_ref, v_ref, o_ref, *scratch, n_lat, n_all, tq, lam_init):
    i = pl.program_id(2)
    lf = lam_ref[...]
    lam = (jnp.exp(jnp.sum(lf[0:1] * lf[1:2], axis=-1, keepdims=True))
           - jnp.exp(jnp.sum(lf[2:3] * lf[3:4], axis=-1, keepdims=True)) + lam_init)
    q = q_ref[0]
    lane = _lane_iota((tq, LANES))
    q_scr, v_scr = scratch[0], scratch[-1]
    for n in range(LANES // DIFF_DQ):
        start = n * DIFF_DQ
        mine = jnp.logical_and(lane >= start, lane < start + DIFF_DQ)
        q_scr[n * tq:(n + 1) * tq, :] = jnp.where(mine, q, jnp.zeros_like(q))

    @pl.when(i == 0)
    def _():
        v = v_ref[0]
        low = _lane_iota(v.shape) < HEAD_DIM
        v_scr[0] = _values_with_ones(v, low)
        v_scr[1] = _values_with_ones(v, jnp.logical_not(low))

    out = _flash_attention(k_ref, scratch, [(0, 2 * tq, 0), (2 * tq, 4 * tq, 1)], i * tq < n_lat, n_lat, n_all)
    o = jnp.where(lane < HEAD_DIM, out[0:tq] - lam * out[tq:2 * tq], out[2 * tq:3 * tq] - lam * out[3 * tq:4 * tq])
    ms = _dot_split(o * o, ones_ref[...])
    o_ref[0] = (o * lax.rsqrt(ms + RMS_EPS) * g_ref[...] * (1.0 - lam_init)).astype(BF16)


def _diff_attention(proj, lam_p, norm_g, ones_bd, n_lat, layer, n_out):
    bsz, n_all, _ = proj.shape
    tq = ATTN_Q_TILE
    lam_init = 0.8 - 0.6 * math.exp(-0.3 * layer)
    qb, kb, vb = COL_DQ // LANES, COL_DK // LANES, COL_DV // LANES
    return pl.pallas_call(
        functools.partial(_diff_kernel, n_lat=n_lat, n_all=n_all, tq=tq, lam_init=lam_init),
        grid=(bsz, 2, n_out // tq),
        in_specs=[pl.BlockSpec(lam_p.shape, lambda b, p, i: (0, 0)),
                  pl.BlockSpec((1, LANES), lambda b, p, i: (0, 0)),
                  pl.BlockSpec((LANES, LANES), lambda b, p, i: (0, 0)),
                  pl.BlockSpec((1, tq, LANES), lambda b, p, i: (b, i, qb + p)),
                  pl.BlockSpec((1, n_all, LANES), lambda b, p, i: (b, 0, kb + p)),
                  pl.BlockSpec((1, n_all, LANES), lambda b, p, i: (b, 0, vb + p))],
        out_specs=pl.BlockSpec((1, tq, LANES), lambda b, p, i: (b, i, p)),
        out_shape=jax.ShapeDtypeStruct((bsz, n_out, BRANCH_W), BF16),
        scratch_shapes=_flash_scratch(4 * tq, n_all, n_lat, min(ATTN_K_TILE, n_lat // 2), 2),
        compiler_params=_params("parallel", "parallel", "arbitrary"),
        name="diff_attention",
    )(lam_p, norm_g, ones_bd, proj, proj, proj)


def _gqa_kernel(q_ref, k_ref, v_ref, o_ref, *scratch, n_lat, n_all, tq):
    kv = pl.program_id(1)
    i = pl.program_id(2)
    q_scr, v_scr = scratch[0], scratch[-1]
    for g in range(2):
        q_scr[g * tq:(g + 1) * tq, :] = q_ref[0, :, g * LANES:(g + 1) * LANES]

    @pl.when(i == 0)
    def _():
        v = v_ref[0]
        head_of_lane = _lane_iota(v.shape) // HEAD_DIM
        v_scr[0] = _values_with_ones(v, head_of_lane == kv)

    out = _flash_attention(k_ref, scratch, [(0, 2 * tq, 0)], i * tq < n_lat, n_lat, n_all)
    outs = [out[0:tq], out[tq:2 * tq]]
    placed = [jnp.where(kv == g, outs[g], pltpu.roll(outs[g], HEAD_DIM, 1)) for g in range(2)]
    lane = _lane_iota((tq, LANES))
    o_ref[0] = jnp.where(lane < HEAD_DIM, placed[0], placed[1]).astype(BF16)


def _gqa_attention(proj, n_lat, n_out):
    bsz, n_all, _ = proj.shape
    tq = ATTN_Q_TILE
    qb, kb, vb = COL_GQ // (2 * LANES), COL_GK // LANES, COL_GV // LANES
    return pl.pallas_call(
        functools.partial(_gqa_kernel, n_lat=n_lat, n_all=n_all, tq=tq),
        grid=(bsz, 2, n_out // tq),
        in_specs=[pl.BlockSpec((1, tq, 2 * LANES), lambda b, p, i: (b, i, qb + p)),
                  pl.BlockSpec((1, n_all, LANES), lambda b, p, i: (b, 0, kb)),
                  pl.BlockSpec((1, n_all, LANES), lambda b, p, i: (b, 0, vb))],
        out_specs=pl.BlockSpec((1, tq, LANES), lambda b, p, i: (b, i, p)),
        out_shape=jax.ShapeDtypeStruct((bsz, n_out, BRANCH_W), BF16),
        scratch_shapes=_flash_scratch(2 * tq, n_all, n_lat, min(ATTN_K_TILE, n_lat // 2), 1),
        compiler_params=_params("parallel", "parallel", "arbitrary"),
        name="gqa_attention",
    )(proj, proj, proj)


def _nat_bias(rpb):
    kc = NAT_WIN_C
    j = np.arange(GRID_W)
    col_start = np.clip(j - kc // 2, 0, GRID_W - kc)
    valid = (j[None, :] >= col_start[:, None]) & (j[None, :] < col_start[:, None] + kc)
    dc = np.clip(j[None, :] - j[:, None], -(kc - 1), kc - 1) + (kc - 1)
    tab = rpb[:, :, dc]
    tab = jnp.where(jnp.asarray(valid)[None, None], tab.astype(F32), NEG_INF)
    heads = rpb.shape[0]
    slabs = []
    for dr0 in range(NAT_WIN_R):
        blk = tab[:, dr0:dr0 + NAT_WIN_R]
        slabs.append(blk.transpose(0, 2, 1, 3).reshape(heads * GRID_W, NAT_WIN_R * GRID_W))
    return jnp.stack(slabs)


NAT_ROWS_PER_STEP = 4


def _nat_kernel(bias_ref, q_ref, k_ref, v_ref, o_ref, *, n_lat, n_all, n_out):
    w = GRID_W
    rows = n_lat // w
    heads = BRANCH_W // HEAD_DIM
    scale = HEAD_DIM ** -0.5
    lane = _lane_iota((w, BRANCH_W))
    head_of_lane = lane // HEAD_DIM
    k_ctx = k_ref[0, n_lat:n_all, :]
    v_ctx = v_ref[0, n_lat:n_all, :]

    def stacked_queries(start):
        q = q_ref[0, pl.ds(start, w), :]
        return jnp.concatenate([jnp.where(head_of_lane == h, q, jnp.zeros_like(q)) for h in range(heads)], axis=0)

    def unstack(o):
        out = jnp.zeros((w, BRANCH_W), F32)
        for h in range(heads):
            out = jnp.where(head_of_lane == h, o[h * w:(h + 1) * w, :], out)
        return out

    def grid_row(r):
        r0 = jnp.clip(r - NAT_WIN_R // 2, 0, rows - NAT_WIN_R)
        dr0 = r0 - r + (NAT_WIN_R - 1)
        qs = stacked_queries(pl.multiple_of(r * w, w))
        kb = k_ref[0, pl.ds(pl.multiple_of(r0 * w, w), NAT_WIN_R * w), :]
        vb = v_ref[0, pl.ds(pl.multiple_of(r0 * w, w), NAT_WIN_R * w), :]
        s_band = _dot_nt(qs, kb) * scale + bias_ref[dr0]
        s_ctx = _dot_nt(qs, k_ctx) * scale
        m = jnp.maximum(jnp.max(s_band, axis=-1, keepdims=True), jnp.max(s_ctx, axis=-1, keepdims=True))
        p_band = jnp.exp(s_band - m)
        p_ctx = jnp.exp(s_ctx - m)
        total = jnp.sum(p_band, axis=-1, keepdims=True) + jnp.sum(p_ctx, axis=-1, keepdims=True)
        o = (jnp.dot(p_band.astype(BF16), vb, preferred_element_type=F32)
             + jnp.dot(p_ctx.astype(BF16), v_ctx, preferred_element_type=F32)) / total
        o_ref[0, pl.ds(pl.multiple_of(r * w, w), w), :] = unstack(o).astype(BF16)

    def row_group(g, carry):
        for u in range(NAT_ROWS_PER_STEP):
            grid_row(g * NAT_ROWS_PER_STEP + u)
        return carry

    assert rows % NAT_ROWS_PER_STEP == 0
    lax.fori_loop(0, rows // NAT_ROWS_PER_STEP, row_group, 0)

    def ctx_block(cb, carry):
        start = pl.multiple_of(n_lat + cb * w, w)
        qs = stacked_queries(start)
        s = _dot_nt(qs, k_ctx) * scale
        p = jnp.exp(s - jnp.max(s, axis=-1, keepdims=True))
        o = jnp.dot(p.astype(BF16), v_ctx, preferred_element_type=F32) / jnp.sum(p, axis=-1, keepdims=True)
        o_ref[0, pl.ds(start, w), :] = unstack(o).astype(BF16)
        return carry

    lax.fori_loop(0, (n_out - n_lat) // w, ctx_block, 0)


def _nat_attention(proj, bias, n_lat, n_out):
    bsz, n_all, _ = proj.shape
    qb, kb, vb = COL_NQ // BRANCH_W, COL_NK // BRANCH_W, COL_NV // BRANCH_W
    return pl.pallas_call(
        functools.partial(_nat_kernel, n_lat=n_lat, n_all=n_all, n_out=n_out),
        grid=(bsz,),
        in_specs=[pl.BlockSpec(bias.shape, lambda b: (0, 0, 0)),
                  pl.BlockSpec((1, n_all, BRANCH_W), lambda b: (b, 0, qb)),
                  pl.BlockSpec((1, n_all, BRANCH_W), lambda b: (b, 0, kb)),
                  pl.BlockSpec((1, n_all, BRANCH_W), lambda b: (b, 0, vb))],
        out_specs=pl.BlockSpec((1, n_out, BRANCH_W), lambda b: (b, 0, 0)),
        out_shape=jax.ShapeDtypeStruct((bsz, n_out, BRANCH_W), BF16),
        compiler_params=_params("parallel"),
        name="nat_attention",
    )(bias, proj, proj, proj)


def _route(logits):
    lane = _lane_iota(logits.shape)
    lane_f = lane.astype(F32)
    far = float(LANES)
    gl = jnp.where(lane < N_GROUPS, logits, -jnp.inf)
    gmax = jnp.max(gl, axis=-1, keepdims=True)
    grp = jnp.min(jnp.where(gl == gmax, lane_f, far), axis=-1, keepdims=True)
    w_grp = 1.0 / jnp.sum(jnp.exp(gl - gmax), axis=-1, keepdims=True)
    first = N_GROUPS + EXPERTS_PER_GROUP * grp
    el = jnp.where(jnp.logical_and(lane_f >= first, lane_f < first + EXPERTS_PER_GROUP), logits, -jnp.inf)
    v1 = jnp.max(el, axis=-1, keepdims=True)
    i1 = jnp.min(jnp.where(el == v1, lane_f, far), axis=-1, keepdims=True)
    el2 = jnp.where(lane_f == i1, -jnp.inf, el)
    v2 = jnp.max(el2, axis=-1, keepdims=True)
    i2 = jnp.min(jnp.where(el2 == v2, lane_f, far), axis=-1, keepdims=True)
    t = jnp.exp(v2 - v1)
    w1 = w_grp / (1.0 + t)
    w2 = w_grp * t / (1.0 + t)
    ids = jnp.where(lane == 0, i1 - N_GROUPS, jnp.where(lane == 1, i2 - N_GROUPS, 0.0)).astype(jnp.int32)
    wts = jnp.where(lane == 0, w1, jnp.where(lane == 1, w2, 0.0))
    return ids, wts


def _merge_kernel(xa_ref, xb_ref, mod_ref, gm_ref, gf_ref, up_ref, u_ref, un_ref, pw_ref, ps_ref,
                  b1_ref, b2_ref, b3_ref, wg_ref, wb_ref, wo_ref, wrh_ref, wrl_ref, br_ref,
                  x1_ref, h2_ref, ids_ref, wts_ref, pad_ref, *, n_lat, n_all):
    d = xa_ref.shape[-1]
    lat_tiles = n_lat // TOKEN_TILE
    x = _token_tile(xa_ref, xb_ref, lat_tiles)
    mod = lambda k: mod_ref[0, 0, k:k + 1, :]
    h = _modulated_norm(x, gm_ref[...], mod(0), mod(1)).astype(BF16)
    pooled = _pool_branch(up_ref, u_ref, un_ref, pw_ref, ps_ref, pad_ref, n_lat, n_all)
    merged = None
    for i, branch in enumerate((pooled, b1_ref[0], b2_ref[0], b3_ref[0])):
        gate = jax.nn.sigmoid(jnp.dot(h, wg_ref[:, i * d:(i + 1) * d], preferred_element_type=F32))
        term = gate * jnp.dot(branch, wb_ref[i], preferred_element_type=F32)
        merged = term if merged is None else merged + term
    o = jnp.dot(merged.astype(BF16), wo_ref[...], preferred_element_type=F32)
    x1 = x + mod(2) * o
    x1_ref[0] = x1
    h2 = _modulated_norm(x1, gf_ref[...], mod(3), mod(4))
    h2_ref[0] = _pack_halves(h2)
    logits = _dot_split(h2, wrh_ref[...], wrl_ref[...]) + br_ref[...]
    ids, wts = _route(logits)
    ids_ref[0] = ids
    wts_ref[0] = wts


def _merge(tokens, n_out, modv, g_mix, g_ffn, u, pool_w, pool_scale, branches, w_gate, w_branch, w_out, w_router,
           b_router, n_lat):
    bsz, _, d = tokens[0].shape
    n_all = u.shape[1]
    tm = TOKEN_TILE
    lat_tiles = n_lat // tm
    tile = lambda width: pl.BlockSpec((1, tm, width), lambda b, t: (b, t, 0))
    full = lambda shape: pl.BlockSpec(shape, lambda b, t: (0,) * len(shape))
    return pl.pallas_call(
        functools.partial(_merge_kernel, n_lat=n_lat, n_all=n_all),
        grid=(bsz, n_out // tm),
        in_specs=[*_token_specs(tokens, tm, lat_tiles),
                  pl.BlockSpec((1, 1, 6, d), lambda b, t: (b, jnp.minimum(t // lat_tiles, 1), 0, 0)),
                  full((1, d)), full((1, d)),
                  *_pool_specs(u, tm), full(pool_w.shape), full(pool_scale.shape),
                  tile(BRANCH_W), tile(BRANCH_W), tile(BRANCH_W),
                  full(w_gate.shape), full(w_branch.shape), full(w_out.shape),
                  full(w_router.shape), full(w_router.shape), full(b_router.shape)],
        out_specs=[tile(d), tile(d // 2), tile(LANES), tile(LANES)],
        out_shape=[jax.ShapeDtypeStruct((bsz, n_out, d), F32), jax.ShapeDtypeStruct((bsz, n_out, d // 2), jnp.int32),
                   jax.ShapeDtypeStruct((bsz, n_out, LANES), jnp.int32),
                   jax.ShapeDtypeStruct((bsz, n_out, LANES), F32)],
        scratch_shapes=[pltpu.VMEM((tm + 2 * POOL_HALO, BRANCH_W), F32)],
        compiler_params=_params("parallel", "parallel"),
        name="merge_router",
    )(tokens[0], tokens[1], modv, g_mix, g_ffn, u, u, u, pool_w, pool_scale, *branches, w_gate, w_branch, w_out,
      *_split_bf16(w_router), b_router)


def _slot_plan(ids):
    te = EXPERT_TILE
    flat = ids.reshape(-1)
    n_pairs = flat.shape[0]
    n_tiles = n_pairs // te + N_EXPERTS
    onehot = (flat[:, None] == jnp.arange(N_EXPERTS, dtype=jnp.int32)[None, :]).astype(jnp.int32)
    running = jnp.cumsum(onehot, axis=0)
    counts = running[-1]
    rank = jnp.sum(running * onehot, axis=1) - 1
    padded = ((counts + te - 1) // te) * te
    ends = jnp.cumsum(padded)
    offsets = ends - padded
    dest = (jnp.sum(onehot * offsets[None, :], axis=1) + rank).astype(jnp.int32)
    tile_start = jnp.arange(n_tiles, dtype=jnp.int32) * te
    tile_expert = jnp.minimum(jnp.sum((tile_start[:, None] >= ends[None, :]).astype(jnp.int32), axis=1),
                              N_EXPERTS - 1).astype(jnp.int32)
    n_used = (ends[-1] // te).astype(jnp.int32).reshape(1)
    return dest, tile_expert, n_used, n_tiles


SUBLANES = 8
ROW_DMA_UNROLL = SUBLANES


def _hbm_row(ref, row):
    return ref.at[pl.ds(row, 1), :]


def _vmem_row(ref, group, sub):
    return ref.at[group, pl.ds(sub, 1), :]


def _wait_rows(make_copy, n_rows):
    def wait(r, carry):
        for _ in range(ROW_DMA_UNROLL):
            make_copy().wait()
        return carry

    lax.fori_loop(0, n_rows // ROW_DMA_UNROLL, wait, 0)


def _dispatch_kernel(dest_ref, h_ref, init_ref, out_ref, stage, stage_sems, row_sem, *, tm):
    del init_ref
    i = pl.program_id(0)
    n_steps = pl.num_programs(0)
    slot = i % 2
    groups = tm // SUBLANES

    def stage_copy(tile, into):
        return pltpu.make_async_copy(h_ref.at[pl.ds(tile * groups, groups)], stage.at[into], stage_sems.at[into])

    def row_copy(group, sub, dst_row):
        return pltpu.make_async_copy(_vmem_row(stage.at[slot], group, sub), _hbm_row(out_ref, dst_row), row_sem)

    wait_all = functools.partial(_wait_rows, lambda: row_copy(0, 0, 0), 2 * tm)

    @pl.when(i == 0)
    def _():
        stage_copy(0, 0).start()

    pl.when(i > 0)(wait_all)

    @pl.when(i + 1 < n_steps)
    def _():
        stage_copy(i + 1, 1 - slot).start()

    stage_copy(i, slot).wait()

    def start(g, carry):
        for u in range(SUBLANES):
            for k in range(2):
                row_copy(g, u, dest_ref[0, 0, 2 * (g * SUBLANES + u) + k]).start()
        return carry

    lax.fori_loop(0, groups, start, 0)
    pl.when(i == n_steps - 1)(wait_all)


def _dispatch(dest, h2, n_slots):
    n_tok, d = h2.shape
    tm = TOKEN_TILE
    groups = tm // SUBLANES
    return pl.pallas_call(
        functools.partial(_dispatch_kernel, tm=tm),
        grid=(n_tok // tm,),
        in_specs=[pl.BlockSpec((1, 1, 2 * tm), lambda i: (i, 0, 0), memory_space=pltpu.SMEM),
                  pl.BlockSpec(memory_space=pl.ANY),
                  pl.BlockSpec(memory_space=pl.ANY)],
        out_specs=pl.BlockSpec(memory_space=pl.ANY),
        out_shape=jax.ShapeDtypeStruct((n_slots, d), h2.dtype),
        scratch_shapes=[pltpu.VMEM((2, groups, SUBLANES, d), h2.dtype), pltpu.SemaphoreType.DMA((2,)),
                        pltpu.SemaphoreType.DMA(())],
        input_output_aliases={2: 0},
        compiler_params=_params("arbitrary"),
        name="moe_dispatch",
    )(dest.reshape(n_tok // tm, 1, 2 * tm), h2.reshape(n_tok // SUBLANES, SUBLANES, d),
      jnp.zeros((n_slots, d), h2.dtype))


def _ffn_kernel(te_ref, used_ref, x_ref, wg_ref, wu_ref, wd_ref, o_ref, wg_scr, wu_scr, wd_scr):
    i = pl.program_id(0)
    used = i < used_ref[0]
    new_expert = jnp.logical_or(i == 0, te_ref[i] != te_ref[jnp.maximum(i - 1, 0)])

    @pl.when(jnp.logical_and(used, new_expert))
    def _():
        wg_scr[...] = wg_ref[0].astype(BF16)
        wu_scr[...] = wu_ref[0].astype(BF16)
        wd_scr[...] = wd_ref[0].astype(BF16)

    @pl.when(used)
    def _():
        hi, lo = _unpack_halves(x_ref[...])
        x = jnp.concatenate([hi.astype(BF16), lo.astype(BF16)], axis=1)
        g = jnp.dot(x, wg_scr[...], preferred_element_type=F32)
        u = jnp.dot(x, wu_scr[...], preferred_element_type=F32)
        a = (g * jax.nn.sigmoid(g) * u).astype(BF16)
        o_ref[...] = _pack_halves(jnp.dot(a, wd_scr[...], preferred_element_type=F32))

    @pl.when(jnp.logical_not(used))
    def _():
        o_ref[...] = jnp.zeros_like(o_ref)


def _expert_ffn(tile_expert, n_used, xs, w_gate, w_up, w_down, n_tiles, layer):
    n_slots, half = xs.shape
    d = 2 * half
    te = EXPERT_TILE
    de = w_gate.shape[-1]
    grid_spec = pltpu.PrefetchScalarGridSpec(
        num_scalar_prefetch=2,
        grid=(n_tiles,),
        in_specs=[pl.BlockSpec((te, half), lambda i, te_ref, used: (i, 0)),
                  pl.BlockSpec((None, 1, d, de), lambda i, te_ref, used: (layer, te_ref[i], 0, 0)),
                  pl.BlockSpec((None, 1, d, de), lambda i, te_ref, used: (layer, te_ref[i], 0, 0)),
                  pl.BlockSpec((None, 1, de, d), lambda i, te_ref, used: (layer, te_ref[i], 0, 0))],
        out_specs=pl.BlockSpec((te, half), lambda i, te_ref, used: (i, 0)),
        scratch_shapes=[pltpu.VMEM((d, de), BF16), pltpu.VMEM((d, de), BF16), pltpu.VMEM((de, d), BF16)],
    )
    return pl.pallas_call(
        _ffn_kernel,
        grid_spec=grid_spec,
        out_shape=jax.ShapeDtypeStruct((n_slots, half), jnp.int32),
        compiler_params=_params("arbitrary"),
        name="moe_ffn",
    )(tile_expert, n_used, xs, w_gate, w_up, w_down)


def _combine_kernel(dest_ref, next_ref, wts_ref, x1_ref, mod_ref, gf_ref, ys_ref, o_ref, buf, sems, *, final):
    tm = x1_ref.shape[1]
    step = pl.program_id(0) * pl.num_programs(1) + pl.program_id(1)
    n_steps = pl.num_programs(0) * pl.num_programs(1)
    slot = step % 2

    def row_copy(src_row, into, k, group, sub):
        return pltpu.make_async_copy(_hbm_row(ys_ref, src_row), _vmem_row(buf.at[into, k], group, sub),
                                     sems.at[into])

    def gather(idx_ref, into):
        def start(g, carry):
            for u in range(SUBLANES):
                for k in range(2):
                    row_copy(idx_ref[0, 0, 2 * (g * SUBLANES + u) + k], into, k, g, u).start()
            return carry

        lax.fori_loop(0, tm // SUBLANES, start, 0)

    @pl.when(step == 0)
    def _():
        gather(dest_ref, 0)

    @pl.when(step + 1 < n_steps)
    def _():
        gather(next_ref, 1 - slot)

    _wait_rows(lambda: row_copy(0, slot, 0, 0, 0), 2 * tm)
    wts = wts_ref[0]
    parts = []
    for k in range(2):
        hi, lo = _unpack_halves(buf[slot, k].reshape(tm, buf.shape[-1]))
        parts.append(wts[:, k:k + 1] * jnp.concatenate([hi, lo], axis=1))
    x2 = x1_ref[0] + mod_ref[0, 0, 5:6, :] * (parts[0] + parts[1])
    if final:
        ms = jnp.mean(x2 * x2, axis=-1, keepdims=True)
        x2 = x2 * lax.rsqrt(ms + RMS_EPS) * gf_ref[...]
    o_ref[0] = x2


def _combine(dest, wts, x1, modv, g_final, ys, n_lat, final):
    bsz, n_all, d = x1.shape
    tm = TOKEN_TILE
    lat_tiles = n_lat // tm
    all_tiles = n_all // tm
    tiles = lat_tiles if final else all_tiles
    n_out = n_lat if final else n_all

    def dest_row(step):
        step = jnp.minimum(step, bsz * tiles - 1)
        return (step // tiles) * all_tiles + step % tiles

    idx_spec = lambda ahead: pl.BlockSpec((1, 1, 2 * tm), lambda b, t: (dest_row(b * tiles + t + ahead), 0, 0),
                                          memory_space=pltpu.SMEM)
    dest3 = dest.reshape(bsz * all_tiles, 1, 2 * tm)
    return pl.pallas_call(
        functools.partial(_combine_kernel, final=final),
        grid=(bsz, tiles),
        in_specs=[idx_spec(0), idx_spec(1),
                  pl.BlockSpec((1, tm, LANES), lambda b, t: (b, t, 0)),
                  pl.BlockSpec((1, tm, d), lambda b, t: (b, t, 0)),
                  pl.BlockSpec((1, 1, 6, d), lambda b, t: (b, jnp.minimum(t // lat_tiles, 1), 0, 0)),
                  pl.BlockSpec((1, d), lambda b, t: (0, 0)),
                  pl.BlockSpec(memory_space=pl.ANY)],
        out_specs=pl.BlockSpec((1, tm, d), lambda b, t: (b, t, 0)),
        out_shape=jax.ShapeDtypeStruct((bsz, n_out, d), F32),
        scratch_shapes=[pltpu.VMEM((2, 2, tm // SUBLANES, SUBLANES, d // 2), jnp.int32),
                        pltpu.SemaphoreType.DMA((2,))],
        compiler_params=_params("arbitrary", "arbitrary"),
        name="moe_combine",
    )(dest3, dest3, wts, x1, modv, g_final, ys)


def kernel(x, c, ctx, c_ctx, w_mod, b_mod, g_mix, g_ffn, w_in, pool_w, pool_scale, diff_lambda, diff_norm_g, nat_rpb, gqa_q_norm, gqa_k_norm, w_branch, w_out, w_router_group, b_router_group, w_router_expert, b_router_expert, w_exp_gate, w_exp_up, w_exp_down, g_final):
    bsz, n_lat, d = x.shape
    n_ctx = ctx.shape[1]
    n_all = n_lat + n_ctx
    depth = w_mod.shape[0]
    tm = TOKEN_TILE
    assert n_lat % tm == 0 and n_ctx % tm == 0 and n_lat % GRID_W == 0 and n_lat // GRID_W >= NAT_WIN_R
    assert d == w_in.shape[1] and w_in.shape[2] == PROJ_COLS + N_BRANCH * d

    tokens = (x, ctx, 0)
    mod_rows = 16
    assert bsz + 1 <= mod_rows
    cc = jnp.zeros((mod_rows, d), F32).at[:bsz].set(c).at[bsz].set(c_ctx)
    mod = _modulation(cc, w_mod, b_mod)
    mod_lat = mod[:, :bsz].reshape(depth, bsz, 1, 6, d)
    mod_ctx = jnp.broadcast_to(mod[:, bsz].reshape(depth, 1, 1, 6, d), (depth, bsz, 1, 6, d))
    modv = jnp.concatenate([mod_lat, mod_ctx], axis=2)

    tabs = _rope_tables(n_lat, n_ctx, 256, DIFF_DQ) + _rope_tables(n_lat, n_ctx, 256, HEAD_DIM)
    head_of = np.arange(LANES) // HEAD_DIM
    ones_bd = jnp.asarray((head_of[:, None] == head_of[None, :]).astype(np.float32) / HEAD_DIM, BF16)
    group_of = np.arange(BRANCH_W) // POOL_GW
    pool_mask = jnp.asarray(group_of[:, None] == group_of[None, :])

    out = None
    for l in range(depth):
        final = l == depth - 1
        n_out = n_lat if final else n_all
        w_proj = w_in[l, :, :PROJ_COLS].astype(BF16)
        w_gate = w_in[l, :, PROJ_COLS:].astype(BF16)
        tile2 = lambda v: jnp.tile(v.reshape(1, HEAD_DIM), (1, LANES // HEAD_DIM))
        u, proj = _inproj(tokens, n_all, modv[l], g_mix[l].reshape(1, d), w_proj, tabs, ones_bd,
                          tile2(gqa_q_norm[l]), tile2(gqa_k_norm[l]), n_lat)
        pool_bd = jnp.where(pool_mask, jnp.tile(pool_w[l].reshape(BRANCH_W, POOL_GW), (1, BRANCH_W // POOL_GW)),
                            0.0).astype(BF16)
        b_diff = _diff_attention(proj, diff_lambda[l], tile2(diff_norm_g[l]), ones_bd, n_lat, l, n_out)
        b_nat = _nat_attention(proj, _nat_bias(nat_rpb[l]), n_lat, n_out)
        b_gqa = _gqa_attention(proj, n_lat, n_out)
        n_route = N_GROUPS + N_EXPERTS
        w_router = jnp.zeros((d, LANES), F32).at[:, :N_GROUPS].set(w_router_group[l])
        w_router = w_router.at[:, N_GROUPS:n_route].set(w_router_expert[l])
        b_router = jnp.zeros((1, LANES), F32).at[0, :N_GROUPS].set(b_router_group[l])
        b_router = b_router.at[0, N_GROUPS:n_route].set(b_router_expert[l])
        x1, h2, ids, wts = _merge(tokens, n_out, modv[l], g_mix[l].reshape(1, d), g_ffn[l].reshape(1, d),
                                  u, pool_bd, pool_scale[l].reshape(1, BRANCH_W),
                                  (b_diff, b_nat, b_gqa), w_gate, w_branch[l].astype(BF16),
                                  w_out[l].astype(BF16), w_router, b_router, n_lat)
        dest, tile_expert, n_used, n_tiles = _slot_plan(ids[:, :, :2])
        xs = _dispatch(dest, h2.reshape(bsz * n_out, d // 2), n_tiles * EXPERT_TILE)
        ys = _expert_ffn(tile_expert, n_used, xs, w_exp_gate, w_exp_up, w_exp_down, n_tiles, l)
        out = _combine(dest, wts, x1, modv[l], g_final.reshape(1, d), ys, n_lat, final)
        tokens = (out, out, n_lat // tm)
    return out
```

```python
import functools
import math

import numpy as np
import jax
import jax.numpy as jnp
from jax import lax
from jax.experimental import pallas as pl
from jax.experimental.pallas import tpu as pltpu

GRID_W = 64
HEAD_DIM = 64
BRANCH_W = 256
N_BRANCH = 4
POOL_HALF_WINDOWS = (1, 2, 4, 8)
POOL_GW = 64
DIFF_DQ = 32
NAT_WIN_R = 8
NAT_WIN_C = 16
N_GROUPS = 4
EXPERTS_PER_GROUP = 8
N_EXPERTS = 32
D_EXPERT = 512
ROPE_THETA = 10000.0
RMS_EPS = 1e-6
NEG_INF = -1e30
LOG2E = 1.4426950408889634

LANES = 128
TOKEN_TILE = 256
EXPERT_TILE = 512
VMEM_LIMIT_BYTES = 56 * 1024 * 1024

PROJ_COLS = 2304
COL_DQ, COL_DK, COL_DV = 0, 256, 512
COL_NQ, COL_NK, COL_NV = 768, 1024, 1280
COL_GQ, COL_GK, COL_GV = 1536, 2048, 2176

F32 = jnp.float32
BF16 = jnp.bfloat16
HIGHEST = lax.Precision.HIGHEST


def _params(*sem):
    return pltpu.CompilerParams(dimension_semantics=sem, vmem_limit_bytes=VMEM_LIMIT_BYTES)


def _lane_iota(shape):
    return lax.broadcasted_iota(jnp.int32, shape, len(shape) - 1)


def _dot_nt(a, b):
    return lax.dot_general(a, b, (((1,), (1,)), ((), ())), preferred_element_type=F32)


def _split_bf16(x):
    hi = x.astype(BF16)
    return hi, (x - hi.astype(F32)).astype(BF16)


def _dot_split(a, b_hi, b_lo=None):
    a_hi, a_lo = _split_bf16(a)
    out = jnp.dot(a_hi, b_hi, preferred_element_type=F32) + jnp.dot(a_lo, b_hi, preferred_element_type=F32)
    if b_lo is not None:
        out = out + jnp.dot(a_hi, b_lo, preferred_element_type=F32)
    return out


def _pack_halves(y):
    w = y.shape[1] // 2
    hi = pltpu.bitcast(y[:, :w].astype(BF16).astype(F32), jnp.int32)
    lo = pltpu.bitcast(y[:, w:].astype(BF16).astype(F32), jnp.int32)
    return jnp.bitwise_or(hi, lax.shift_right_logical(lo, 16))


def _unpack_halves(word):
    hi = pltpu.bitcast(jnp.bitwise_and(word, -65536), F32)
    lo = pltpu.bitcast(lax.shift_left(word, 16), F32)
    return hi, lo


def _modulated_norm(x, gain, shift, scale):
    ms = jnp.mean(x * x, axis=-1, keepdims=True)
    return (x * lax.rsqrt(ms + RMS_EPS)) * gain * (1.0 + scale) + shift


def _mod_kernel(c_ref, w_ref, b_ref, o_ref):
    c = c_ref[...]
    s = c * jax.nn.sigmoid(c)
    o_ref[0] = jnp.dot(s, w_ref[0], preferred_element_type=F32, precision=HIGHEST) + b_ref[0]


def _modulation(cc, w_mod, b_mod):
    depth, d, n6 = w_mod.shape
    rows = cc.shape[0]
    tn = 512
    return pl.pallas_call(
        _mod_kernel,
        grid=(depth, n6 // tn),
        in_specs=[pl.BlockSpec((rows, d), lambda l, j: (0, 0)),
                  pl.BlockSpec((1, d, tn), lambda l, j: (l, 0, j)),
                  pl.BlockSpec((1, 1, tn), lambda l, j: (l, 0, j))],
        out_specs=pl.BlockSpec((1, rows, tn), lambda l, j: (l, 0, j)),
        out_shape=jax.ShapeDtypeStruct((depth, rows, n6), F32),
        compiler_params=_params("arbitrary", "arbitrary"),
        name="modulation",
    )(cc, w_mod, b_mod.reshape(depth, 1, n6))


def _rope_tables(n_lat, n_ctx, width, vec_dim):
    t = np.arange(n_lat)
    row, col = t // GRID_W, t % GRID_W
    j = np.arange(width) % vec_dim
    half = vec_dim // 2
    quarter = half // 2
    jj = j % half
    inv = ROPE_THETA ** (-(jj % quarter).astype(np.float64) / quarter)
    pos = np.where((j < half)[None, :], row[:, None], col[:, None]).astype(np.float64)
    ang = pos * inv[None, :]
    sign = np.where(jj < quarter, -1.0, 1.0)[None, :]
    cos = np.concatenate([np.cos(ang), np.ones((n_ctx, width))], axis=0)
    sin = np.concatenate([np.sin(ang) * sign, np.zeros((n_ctx, width))], axis=0)
    return jnp.asarray(cos, F32), jnp.asarray(sin, F32)


def _rope(x, cos, sin, quarter):
    lane = _lane_iota(x.shape)
    first = (lane % (2 * quarter)) < quarter
    up = pltpu.roll(x, LANES - quarter, 1)
    down = pltpu.roll(x, quarter, 1)
    return x * cos + jnp.where(first, up, down) * sin


def _head_rms(x, ones_bd, gain):
    ms = _dot_split(x * x, ones_bd)
    return x * lax.rsqrt(ms + RMS_EPS) * gain


def _token_specs(tokens, tm, lat_tiles):
    lat, ctx, ctx_first = tokens
    d = lat.shape[-1]
    return [pl.BlockSpec((1, tm, d), lambda b, t: (b, jnp.minimum(t, lat_tiles - 1), 0)),
            pl.BlockSpec((1, tm, d), lambda b, t: (b, ctx_first + jnp.maximum(t - lat_tiles, 0), 0))]


def _token_tile(xa_ref, xb_ref, lat_tiles):
    return jnp.where(pl.program_id(1) >= lat_tiles, xb_ref[0], xa_ref[0])


def _inproj_kernel(xa_ref, xb_ref, mod_ref, g_ref, w_ref, cd_ref, sd_ref, cg_ref, sg_ref, ones_ref, qn_ref, kn_ref,
                   u_ref, p_ref, *, lat_tiles):
    x = _token_tile(xa_ref, xb_ref, lat_tiles)
    h = _modulated_norm(x, g_ref[...], mod_ref[0, 0, 0:1, :], mod_ref[0, 0, 1:2, :]).astype(BF16)

    def proj(col):
        y = jnp.dot(h, w_ref[:, col:col + 2 * LANES], preferred_element_type=F32)
        return y[:, 0:LANES], y[:, LANES:2 * LANES]

    for b, y in enumerate(proj(0)):
        u_ref[0, :, b * LANES:(b + 1) * LANES] = y
    for dst, src, mult in ((COL_DQ, 256, DIFF_DQ ** -0.5 * LOG2E), (COL_DK, 512, 1.0)):
        for b, y in enumerate(proj(src)):
            sl = slice(b * LANES, (b + 1) * LANES)
            y = _rope(y, cd_ref[:, sl], sd_ref[:, sl], DIFF_DQ // 4) * mult
            p_ref[0, :, dst + b * LANES:dst + (b + 1) * LANES] = y.astype(BF16)
    for dst, src in ((COL_DV, 768), (COL_NQ, 1024), (COL_NK, 1280), (COL_NV, 1536)):
        for b, y in enumerate(proj(src)):
            p_ref[0, :, dst + b * LANES:dst + (b + 1) * LANES] = y.astype(BF16)
    ones_bd = ones_ref[...]
    lane = _lane_iota((x.shape[0], LANES))
    low = lane < HEAD_DIM
    for b, y in enumerate(proj(1792)):
        sl = slice(b * LANES, (b + 1) * LANES)
        y = _head_rms(y, ones_bd, qn_ref[...])
        y = _rope(y, cg_ref[:, sl], sg_ref[:, sl], HEAD_DIM // 4) * (HEAD_DIM ** -0.5 * LOG2E)
        swapped = pltpu.roll(y, HEAD_DIM, 1)
        keep = low if b == 0 else jnp.logical_not(low)
        even = jnp.where(keep, y if b == 0 else swapped, 0.0)
        odd = jnp.where(keep, swapped if b == 0 else y, 0.0)
        p_ref[0, :, COL_GQ + (2 * b) * LANES:COL_GQ + (2 * b + 1) * LANES] = even.astype(BF16)
        p_ref[0, :, COL_GQ + (2 * b + 1) * LANES:COL_GQ + (2 * b + 2) * LANES] = odd.astype(BF16)
    gk, gv = proj(2048)
    y = _head_rms(gk, ones_bd, kn_ref[...])
    y = _rope(y, cg_ref[:, 0:LANES], sg_ref[:, 0:LANES], HEAD_DIM // 4)
    p_ref[0, :, COL_GK:COL_GK + LANES] = y.astype(BF16)
    p_ref[0, :, COL_GV:COL_GV + LANES] = gv.astype(BF16)


def _inproj(tokens, n_all, modv, g_mix, w_proj, tabs, ones_bd, q_norm, k_norm, n_lat):
    bsz, _, d = tokens[0].shape
    tm = TOKEN_TILE
    lat_tiles = n_lat // tm
    cd, sd, cg, sg = tabs
    full = lambda shape: pl.BlockSpec(shape, lambda b, t: (0,) * len(shape))
    tab = pl.BlockSpec((tm, 256), lambda b, t: (t, 0))
    return pl.pallas_call(
        functools.partial(_inproj_kernel, lat_tiles=lat_tiles),
        grid=(bsz, n_all // tm),
        in_specs=[*_token_specs(tokens, tm, lat_tiles),
                  pl.BlockSpec((1, 1, 6, d), lambda b, t: (b, jnp.minimum(t // lat_tiles, 1), 0, 0)),
                  full((1, d)), full((d, PROJ_COLS)), tab, tab, tab, tab,
                  full((LANES, LANES)), full((1, LANES)), full((1, LANES))],
        out_specs=[pl.BlockSpec((1, tm, BRANCH_W), lambda b, t: (b, t, 0)),
                   pl.BlockSpec((1, tm, PROJ_COLS), lambda b, t: (b, t, 0))],
        out_shape=[jax.ShapeDtypeStruct((bsz, n_all, BRANCH_W), F32),
                   jax.ShapeDtypeStruct((bsz, n_all, PROJ_COLS), BF16)],
        compiler_params=_params("parallel", "parallel"),
        name="inproj",
    )(tokens[0], tokens[1], modv, g_mix, w_proj, cd, sd, cg, sg, ones_bd, q_norm, k_norm)


POOL_HALO = 8


def _pool_specs(u, tm):
    _, n_all, ch = u.shape
    per = tm // POOL_HALO
    nblk = n_all // POOL_HALO
    return [pl.BlockSpec((1, POOL_HALO, ch), lambda b, t: (b, jnp.maximum(t * per - 1, 0), 0)),
            pl.BlockSpec((1, tm, ch), lambda b, t: (b, t, 0)),
            pl.BlockSpec((1, POOL_HALO, ch), lambda b, t: (b, jnp.minimum((t + 1) * per, nblk - 1), 0))]


def _pool_branch(prev_ref, u_ref, next_ref, w_ref, scale_ref, pad_ref, n_lat, n_all):
    t = pl.program_id(1)
    tm = TOKEN_TILE
    lat_tiles = n_lat // tm
    all_tiles = n_all // tm
    halo = POOL_HALO
    u = u_ref[0]
    first = jnp.logical_or(t == 0, t == lat_tiles)
    last = jnp.logical_or(t == lat_tiles - 1, t == all_tiles - 1)
    pad_ref[0:halo, :] = jnp.where(first, 0.0, prev_ref[0])
    pad_ref[halo:halo + tm, :] = u
    pad_ref[halo + tm:2 * halo + tm, :] = jnp.where(last, 0.0, next_ref[0])

    def shifted(d):
        return pad_ref[halo + d:halo + d + tm, :]

    sums = []
    acc = None
    for hw in POOL_HALF_WINDOWS:
        lo = hw // 2 if acc is not None else 0
        for d in range(lo, hw):
            term = shifted(d) + shifted(-d - 1)
            acc = term if acc is None else acc + term
        sums.append(acc)
    lane = _lane_iota((tm, BRANCH_W))
    group = lane // POOL_GW
    win = jnp.where(group == 0, sums[0], jnp.where(group == 1, sums[1], jnp.where(group == 2, sums[2], sums[3])))
    half = jnp.where(group == 0, 1, jnp.where(group == 1, 2, jnp.where(group == 2, 4, 8)))
    seg_start = jnp.where(t >= lat_tiles, n_lat, 0)
    seg_len = jnp.where(t >= lat_tiles, n_all - n_lat, n_lat)
    pos = t * tm - seg_start + lax.broadcasted_iota(jnp.int32, (tm, BRANCH_W), 0)
    cnt = jnp.minimum(pos + half, seg_len) - jnp.maximum(pos - half, 0)
    pooled = (win / cnt.astype(F32) - u).astype(BF16)
    y = jnp.dot(pooled, w_ref[...], preferred_element_type=F32) * scale_ref[...]
    return y.astype(BF16)


ATTN_Q_TILE = 256
ATTN_ROW_BLOCK = 64
ATTN_K_TILE = 2048


def _flash_scratch(rows, n_all, n_lat, tk, n_value_sets):
    wide = lambda: pltpu.VMEM((rows, LANES), F32)
    return [pltpu.VMEM((rows, LANES), BF16), pltpu.VMEM((2, rows, tk), F32),
            pltpu.VMEM((rows, n_all - n_lat), F32), pltpu.VMEM((rows, tk), BF16), wide(), wide(), wide(),
            pltpu.VMEM((n_value_sets, n_all, LANES), BF16)]


def _values_with_ones(v, own_half_mask):
    return jnp.where(own_half_mask, v, jnp.ones_like(v))


def _flash_attention(k_ref, scratch, value_rows, is_latent_block, n_lat, n_all):
    q_scr, s_scr, sc_scr, p_scr, m_scr, a_scr, acc_scr, v_scr = scratch
    rows = q_scr.shape[0]
    tk = p_scr.shape[1]
    rb = ATTN_ROW_BLOCK
    n_chunks = n_lat // tk
    n_ctx = n_all - n_lat
    assert n_chunks % 2 == 0 and n_chunks * tk == n_lat and n_ctx <= tk

    m_scr[...] = jnp.full(m_scr.shape, -jnp.inf, F32)
    acc_scr[...] = jnp.zeros(acc_scr.shape, F32)

    def produce(dst, start, size):
        dst[...] = _dot_nt(q_scr[...], k_ref[0, pl.ds(start, size), :])

    def consume(src, start, size):
        for r in range(rows // rb):
            rs = slice(r * rb, (r + 1) * rb)
            s = src[rs, :]
            m_old = m_scr[rs, :]
            m_new = jnp.maximum(m_old, jnp.broadcast_to(jnp.max(s, axis=-1, keepdims=True), (rb, LANES)))
            for b in range(size // LANES):
                cols = slice(b * LANES, (b + 1) * LANES)
                p_scr[rs, cols] = jnp.exp2(s[:, cols] - m_new).astype(BF16)
            m_scr[rs, :] = m_new
            a_scr[rs, :] = jnp.exp2(m_old - m_new)
        for r0, r1, vset in value_rows:
            pv = jnp.dot(p_scr[r0:r1, 0:size], v_scr[vset, pl.ds(start, size), :], preferred_element_type=F32)
            acc_scr[r0:r1, :] = a_scr[r0:r1, :] * acc_scr[r0:r1, :] + pv

    def lat(chunk):
        return pl.multiple_of(chunk * tk, tk)

    produce(sc_scr, n_lat, n_ctx)

    @pl.when(is_latent_block)
    def _():
        produce(s_scr.at[0], 0, tk)

        def step(i, carry):
            produce(s_scr.at[1], lat(2 * i + 1), tk)
            consume(s_scr.at[0], lat(2 * i), tk)
            produce(s_scr.at[0], lat(2 * i + 2), tk)
            consume(s_scr.at[1], lat(2 * i + 1), tk)
            return carry

        lax.fori_loop(0, n_chunks // 2 - 1, step, 0)
        produce(s_scr.at[1], (n_chunks - 1) * tk, tk)
        consume(s_scr.at[0], (n_chunks - 2) * tk, tk)
        consume(s_scr.at[1], (n_chunks - 1) * tk, tk)

    consume(sc_scr, n_lat, n_ctx)
    acc = acc_scr[...]
    return acc / pltpu.roll(acc, HEAD_DIM, 1)


def _diff_kernel(lam_ref, g_ref, ones_ref, q_ref, k_ref, v_ref, o_ref, *scratch, n_lat, n_all, tq, lam_init):
    i = pl.program_id(2)
    lf = lam_ref[...]
    lam = (jnp.exp(jnp.sum(lf[0:1] * lf[1:2], axis=-1, keepdims=True))
           - jnp.exp(jnp.sum(lf[2:3] * lf[3:4], axis=-1, keepdims=True)) + lam_init)
    q = q_ref[0]
    lane = _lane_iota((tq, LANES))
    q_scr, v_scr = scratch[0], scratch[-1]
    for n in range(LANES // DIFF_DQ):
        start = n * DIFF_DQ
        mine = jnp.logical_and(lane >= start, lane < start + DIFF_DQ)
        q_scr[n * tq:(n + 1) * tq, :] = jnp.where(mine, q, jnp.zeros_like(q))

    @pl.when(i == 0)
    def _():
        v = v_ref[0]
        low = _lane_iota(v.shape) < HEAD_DIM
        v_scr[0] = _values_with_ones(v, low)
        v_scr[1] = _values_with_ones(v, jnp.logical_not(low))

    out = _flash_attention(k_ref, scratch, [(0, 2 * tq, 0), (2 * tq, 4 * tq, 1)], i * tq < n_lat, n_lat, n_all)
    o = jnp.where(lane < HEAD_DIM, out[0:tq] - lam * out[tq:2 * tq], out[2 * tq:3 * tq] - lam * out[3 * tq:4 * tq])
    ms = _dot_split(o * o, ones_ref[...])
    o_ref[0] = (o * lax.rsqrt(ms + RMS_EPS) * g_ref[...] * (1.0 - lam_init)).astype(BF16)


def _diff_attention(proj, lam_p, norm_g, ones_bd, n_lat, layer, n_out):
    bsz, n_all, _ = proj.shape
    tq = ATTN_Q_TILE
    lam_init = 0.8 - 0.6 * math.exp(-0.3 * layer)
    qb, kb, vb = COL_DQ // LANES, COL_DK // LANES, COL_DV // LANES
    return pl.pallas_call(
        functools.partial(_diff_kernel, n_lat=n_lat, n_all=n_all, tq=tq, lam_init=lam_init),
        grid=(bsz, 2, n_out // tq),
        in_specs=[pl.BlockSpec(lam_p.shape, lambda b, p, i: (0, 0)),
                  pl.BlockSpec((1, LANES), lambda b, p, i: (0, 0)),
                  pl.BlockSpec((LANES, LANES), lambda b, p, i: (0, 0)),
                  pl.BlockSpec((1, tq, LANES), lambda b, p, i: (b, i, qb + p)),
                  pl.BlockSpec((1, n_all, LANES), lambda b, p, i: (b, 0, kb + p)),
                  pl.BlockSpec((1, n_all, LANES), lambda b, p, i: (b, 0, vb + p))],
        out_specs=pl.BlockSpec((1, tq, LANES), lambda b, p, i: (b, i, p)),
        out_shape=jax.ShapeDtypeStruct((bsz, n_out, BRANCH_W), BF16),
        scratch_shapes=_flash_scratch(4 * tq, n_all, n_lat, min(ATTN_K_TILE, n_lat // 2), 2),
        compiler_params=_params("parallel", "parallel", "arbitrary"),
        name="diff_attention",
    )(lam_p, norm_g, ones_bd, proj, proj, proj)


def _gqa_kernel(q_ref, k_ref, v_ref, o_ref, *scratch, n_lat, n_all, tq):
    kv = pl.program_id(1)
    i = pl.program_id(2)
    q_scr, v_scr = scratch[0], scratch[-1]
    for g in range(2):
        q_scr[g * tq:(g + 1) * tq, :] = q_ref[0, :, g * LANES:(g + 1) * LANES]

    @pl.when(i == 0)
    def _():
        v = v_ref[0]
        head_of_lane = _lane_iota(v.shape) // HEAD_DIM
        v_scr[0] = _values_with_ones(v, head_of_lane == kv)

    out = _flash_attention(k_ref, scratch, [(0, 2 * tq, 0)], i * tq < n_lat, n_lat, n_all)
    outs = [out[0:tq], out[tq:2 * tq]]
    placed = [jnp.where(kv == g, outs[g], pltpu.roll(outs[g], HEAD_DIM, 1)) for g in range(2)]
    lane = _lane_iota((tq, LANES))
    o_ref[0] = jnp.where(lane < HEAD_DIM, placed[0], placed[1]).astype(BF16)


def _gqa_attention(proj, n_lat, n_out):
    bsz, n_all, _ = proj.shape
    tq = ATTN_Q_TILE
    qb, kb, vb = COL_GQ // (2 * LANES), COL_GK // LANES, COL_GV // LANES
    return pl.pallas_call(
        functools.partial(_gqa_kernel, n_lat=n_lat, n_all=n_all, tq=tq),
        grid=(bsz, 2, n_out // tq),
        in_specs=[pl.BlockSpec((1, tq, 2 * LANES), lambda b, p, i: (b, i, qb + p)),
                  pl.BlockSpec((1, n_all, LANES), lambda b, p, i: (b, 0, kb)),
                  pl.BlockSpec((1, n_all, LANES), lambda b, p, i: (b, 0, vb))],
        out_specs=pl.BlockSpec((1, tq, LANES), lambda b, p, i: (b, i, p)),
        out_shape=jax.ShapeDtypeStruct((bsz, n_out, BRANCH_W), BF16),
        scratch_shapes=_flash_scratch(2 * tq, n_all, n_lat, min(ATTN_K_TILE, n_lat // 2), 1),
        compiler_params=_params("parallel", "parallel", "arbitrary"),
        name="gqa_attention",
    )(proj, proj, proj)


def _nat_bias(rpb):
    kc = NAT_WIN_C
    j = np.arange(GRID_W)
    col_start = np.clip(j - kc // 2, 0, GRID_W - kc)
    valid = (j[None, :] >= col_start[:, None]) & (j[None, :] < col_start[:, None] + kc)
    dc = np.clip(j[None, :] - j[:, None], -(kc - 1), kc - 1) + (kc - 1)
    tab = rpb[:, :, dc]
    tab = jnp.where(jnp.asarray(valid)[None, None], tab.astype(F32), NEG_INF)
    heads = rpb.shape[0]
    slabs = []
    for dr0 in range(NAT_WIN_R):
        blk = tab[:, dr0:dr0 + NAT_WIN_R]
        slabs.append(blk.transpose(0, 2, 1, 3).reshape(heads * GRID_W, NAT_WIN_R * GRID_W))
    return jnp.stack(slabs)


NAT_ROWS_PER_STEP = 4


def _nat_kernel(bias_ref, q_ref, k_ref, v_ref, o_ref, *, n_lat, n_all, n_out):
    w = GRID_W
    rows = n_lat // w
    heads = BRANCH_W // HEAD_DIM
    scale = HEAD_DIM ** -0.5
    lane = _lane_iota((w, BRANCH_W))
    head_of_lane = lane // HEAD_DIM
    k_ctx = k_ref[0, n_lat:n_all, :]
    v_ctx = v_ref[0, n_lat:n_all, :]

    def stacked_queries(start):
        q = q_ref[0, pl.ds(start, w), :]
        return jnp.concatenate([jnp.where(head_of_lane == h, q, jnp.zeros_like(q)) for h in range(heads)], axis=0)

    def unstack(o):
        out = jnp.zeros((w, BRANCH_W), F32)
        for h in range(heads):
            out = jnp.where(head_of_lane == h, o[h * w:(h + 1) * w, :], out)
        return out

    def grid_row(r):
        r0 = jnp.clip(r - NAT_WIN_R // 2, 0, rows - NAT_WIN_R)
        dr0 = r0 - r + (NAT_WIN_R - 1)
        qs = stacked_queries(pl.multiple_of(r * w, w))
        kb = k_ref[0, pl.ds(pl.multiple_of(r0 * w, w), NAT_WIN_R * w), :]
        vb = v_ref[0, pl.ds(pl.multiple_of(r0 * w, w), NAT_WIN_R * w), :]
        s_band = _dot_nt(qs, kb) * scale + bias_ref[dr0]
        s_ctx = _dot_nt(qs, k_ctx) * scale
        m = jnp.maximum(jnp.max(s_band, axis=-1, keepdims=True), jnp.max(s_ctx, axis=-1, keepdims=True))
        p_band = jnp.exp(s_band - m)
        p_ctx = jnp.exp(s_ctx - m)
        total = jnp.sum(p_band, axis=-1, keepdims=True) + jnp.sum(p_ctx, axis=-1, keepdims=True)
        o = (jnp.dot(p_band.astype(BF16), vb, preferred_element_type=F32)
             + jnp.dot(p_ctx.astype(BF16), v_ctx, preferred_element_type=F32)) / total
        o_ref[0, pl.ds(pl.multiple_of(r * w, w), w), :] = unstack(o).astype(BF16)

    def row_group(g, carry):
        for u in range(NAT_ROWS_PER_STEP):
            grid_row(g * NAT_ROWS_PER_STEP + u)
        return carry

    assert rows % NAT_ROWS_PER_STEP == 0
    lax.fori_loop(0, rows // NAT_ROWS_PER_STEP, row_group, 0)

    def ctx_block(cb, carry):
        start = pl.multiple_of(n_lat + cb * w, w)
        qs = stacked_queries(start)
        s = _dot_nt(qs, k_ctx) * scale
        p = jnp.exp(s - jnp.max(s, axis=-1, keepdims=True))
        o = jnp.dot(p.astype(BF16), v_ctx, preferred_element_type=F32) / jnp.sum(p, axis=-1, keepdims=True)
        o_ref[0, pl.ds(start, w), :] = unstack(o).astype(BF16)
        return carry

    lax.fori_loop(0, (n_out - n_lat) // w, ctx_block, 0)


def _nat_attention(proj, bias, n_lat, n_out):
    bsz, n_all, _ = proj.shape
    qb, kb, vb = COL_NQ // BRANCH_W, COL_NK // BRANCH_W, COL_NV // BRANCH_W
    return pl.pallas_call(
        functools.partial(_nat_kernel, n_lat=n_lat, n_all=n_all, n_out=n_out),
        grid=(bsz,),
        in_specs=[pl.BlockSpec(bias.shape, lambda b: (0, 0, 0)),
                  pl.BlockSpec((1, n_all, BRANCH_W), lambda b: (b, 0, qb)),
                  pl.BlockSpec((1, n_all, BRANCH_W), lambda b: (b, 0, kb)),
                  pl.BlockSpec((1, n_all, BRANCH_W), lambda b: (b, 0, vb))],
        out_specs=pl.BlockSpec((1, n_out, BRANCH_W), lambda b: (b, 0, 0)),
        out_shape=jax.ShapeDtypeStruct((bsz, n_out, BRANCH_W), BF16),
        compiler_params=_params("parallel"),
        name="nat_attention",
    )(bias, proj, proj, proj)


def _route(logits):
    lane = _lane_iota(logits.shape)
    lane_f = lane.astype(F32)
    far = float(LANES)
    gl = jnp.where(lane < N_GROUPS, logits, -jnp.inf)
    gmax = jnp.max(gl, axis=-1, keepdims=True)
    grp = jnp.min(jnp.where(gl == gmax, lane_f, far), axis=-1, keepdims=True)
    w_grp = 1.0 / jnp.sum(jnp.exp(gl - gmax), axis=-1, keepdims=True)
    first = N_GROUPS + EXPERTS_PER_GROUP * grp
    el = jnp.where(jnp.logical_and(lane_f >= first, lane_f < first + EXPERTS_PER_GROUP), logits, -jnp.inf)
    v1 = jnp.max(el, axis=-1, keepdims=True)
    i1 = jnp.min(jnp.where(el == v1, lane_f, far), axis=-1, keepdims=True)
    el2 = jnp.where(lane_f == i1, -jnp.inf, el)
    v2 = jnp.max(el2, axis=-1, keepdims=True)
    i2 = jnp.min(jnp.where(el2 == v2, lane_f, far), axis=-1, keepdims=True)
    t = jnp.exp(v2 - v1)
    w1 = w_grp / (1.0 + t)
    w2 = w_grp * t / (1.0 + t)
    ids = jnp.where(lane == 0, i1 - N_GROUPS, jnp.where(lane == 1, i2 - N_GROUPS, 0.0)).astype(jnp.int32)
    wts = jnp.where(lane == 0, w1, jnp.where(lane == 1, w2, 0.0))
    return ids, wts


def _merge_kernel(xa_ref, xb_ref, mod_ref, gm_ref, gf_ref, up_ref, u_ref, un_ref, pw_ref, ps_ref,
                  b1_ref, b2_ref, b3_ref, wg_ref, wb_ref, wo_ref, wrh_ref, wrl_ref, br_ref,
                  x1_ref, h2_ref, ids_ref, wts_ref, pad_ref, *, n_lat, n_all):
    d = xa_ref.shape[-1]
    lat_tiles = n_lat // TOKEN_TILE
    x = _token_tile(xa_ref, xb_ref, lat_tiles)
    mod = lambda k: mod_ref[0, 0, k:k + 1, :]
    h = _modulated_norm(x, gm_ref[...], mod(0), mod(1)).astype(BF16)
    pooled = _pool_branch(up_ref, u_ref, un_ref, pw_ref, ps_ref, pad_ref, n_lat, n_all)
    merged = None
    for i, branch in enumerate((pooled, b1_ref[0], b2_ref[0], b3_ref[0])):
        gate = jax.nn.sigmoid(jnp.dot(h, wg_ref[:, i * d:(i + 1) * d], preferred_element_type=F32))
        term = gate * jnp.dot(branch, wb_ref[i], preferred_element_type=F32)
        merged = term if merged is None else merged + term
    o = jnp.dot(merged.astype(BF16), wo_ref[...], preferred_element_type=F32)
    x1 = x + mod(2) * o
    x1_ref[0] = x1
    h2 = _modulated_norm(x1, gf_ref[...], mod(3), mod(4))
    h2_ref[0] = _pack_halves(h2)
    logits = _dot_split(h2, wrh_ref[...], wrl_ref[...]) + br_ref[...]
    ids, wts = _route(logits)
    ids_ref[0] = ids
    wts_ref[0] = wts


def _merge(tokens, n_out, modv, g_mix, g_ffn, u, pool_w, pool_scale, branches, w_gate, w_branch, w_out, w_router,
           b_router, n_lat):
    bsz, _, d = tokens[0].shape
    n_all = u.shape[1]
    tm = TOKEN_TILE
    lat_tiles = n_lat // tm
    tile = lambda width: pl.BlockSpec((1, tm, width), lambda b, t: (b, t, 0))
    full = lambda shape: pl.BlockSpec(shape, lambda b, t: (0,) * len(shape))
    return pl.pallas_call(
        functools.partial(_merge_kernel, n_lat=n_lat, n_all=n_all),
        grid=(bsz, n_out // tm),
        in_specs=[*_token_specs(tokens, tm, lat_tiles),
                  pl.BlockSpec((1, 1, 6, d), lambda b, t: (b, jnp.minimum(t // lat_tiles, 1), 0, 0)),
                  full((1, d)), full((1, d)),
                  *_pool_specs(u, tm), full(pool_w.shape), full(pool_scale.shape),
                  tile(BRANCH_W), tile(BRANCH_W), tile(BRANCH_W),
                  full(w_gate.shape), full(w_branch.shape), full(w_out.shape),
                  full(w_router.shape), full(w_router.shape), full(b_router.shape)],
        out_specs=[tile(d), tile(d // 2), tile(LANES), tile(LANES)],
        out_shape=[jax.ShapeDtypeStruct((bsz, n_out, d), F32), jax.ShapeDtypeStruct((bsz, n_out, d // 2), jnp.int32),
                   jax.ShapeDtypeStruct((bsz, n_out, LANES), jnp.int32),
                   jax.ShapeDtypeStruct((bsz, n_out, LANES), F32)],
        scratch_shapes=[pltpu.VMEM((tm + 2 * POOL_HALO, BRANCH_W), F32)],
        compiler_params=_params("parallel", "parallel"),
        name="merge_router",
    )(tokens[0], tokens[1], modv, g_mix, g_ffn, u, u, u, pool_w, pool_scale, *branches, w_gate, w_branch, w_out,
      *_split_bf16(w_router), b_router)


def _slot_plan(ids):
    te = EXPERT_TILE
    flat = ids.reshape(-1)
    n_pairs = flat.shape[0]
    n_tiles = n_pairs // te + N_EXPERTS
    onehot = (flat[:, None] == jnp.arange(N_EXPERTS, dtype=jnp.int32)[None, :]).astype(jnp.int32)
    running = jnp.cumsum(onehot, axis=0)
    counts = running[-1]
    rank = jnp.sum(running * onehot, axis=1) - 1
    padded = ((counts + te - 1) // te) * te
    ends = jnp.cumsum(padded)
    offsets = ends - padded
    dest = (jnp.sum(onehot * offsets[None, :], axis=1) + rank).astype(jnp.int32)
    tile_start = jnp.arange(n_tiles, dtype=jnp.int32) * te
    tile_expert = jnp.minimum(jnp.sum((tile_start[:, None] >= ends[None, :]).astype(jnp.int32), axis=1),
                              N_EXPERTS - 1).astype(jnp.int32)
    n_used = (ends[-1] // te).astype(jnp.int32).reshape(1)
    return dest, tile_expert, n_used, n_tiles


SUBLANES = 8
ROW_DMA_UNROLL = SUBLANES


def _hbm_row(ref, row):
    return ref.at[pl.ds(row, 1), :]


def _vmem_row(ref, group, sub):
    return ref.at[group, pl.ds(sub, 1), :]


def _wait_rows(make_copy, n_rows):
    def wait(r, carry):
        for _ in range(ROW_DMA_UNROLL):
            make_copy().wait()
        return carry

    lax.fori_loop(0, n_rows // ROW_DMA_UNROLL, wait, 0)


def _dispatch_kernel(dest_ref, h_ref, init_ref, out_ref, stage, stage_sems, row_sem, *, tm):
    del init_ref
    i = pl.program_id(0)
    n_steps = pl.num_programs(0)
    slot = i % 2
    groups = tm // SUBLANES

    def stage_copy(tile, into):
        return pltpu.make_async_copy(h_ref.at[pl.ds(tile * groups, groups)], stage.at[into], stage_sems.at[into])

    def row_copy(group, sub, dst_row):
        return pltpu.make_async_copy(_vmem_row(stage.at[slot], group, sub), _hbm_row(out_ref, dst_row), row_sem)

    wait_all = functools.partial(_wait_rows, lambda: row_copy(0, 0, 0), 2 * tm)

    @pl.when(i == 0)
    def _():
        stage_copy(0, 0).start()

    pl.when(i > 0)(wait_all)

    @pl.when(i + 1 < n_steps)
    def _():
        stage_copy(i + 1, 1 - slot).start()

    stage_copy(i, slot).wait()

    def start(g, carry):
        for u in range(SUBLANES):
            for k in range(2):
                row_copy(g, u, dest_ref[0, 0, 2 * (g * SUBLANES + u) + k]).start(priority=k)
        return carry

    lax.fori_loop(0, groups, start, 0)
    pl.when(i == n_steps - 1)(wait_all)


def _dispatch(dest, h2, n_slots):
    n_tok, d = h2.shape
    tm = TOKEN_TILE
    groups = tm // SUBLANES
    return pl.pallas_call(
        functools.partial(_dispatch_kernel, tm=tm),
        grid=(n_tok // tm,),
        in_specs=[pl.BlockSpec((1, 1, 2 * tm), lambda i: (i, 0, 0), memory_space=pltpu.SMEM),
                  pl.BlockSpec(memory_space=pl.ANY),
                  pl.BlockSpec(memory_space=pl.ANY)],
        out_specs=pl.BlockSpec(memory_space=pl.ANY),
        out_shape=jax.ShapeDtypeStruct((n_slots, d), h2.dtype),
        scratch_shapes=[pltpu.VMEM((2, groups, SUBLANES, d), h2.dtype), pltpu.SemaphoreType.DMA((2,)),
                        pltpu.SemaphoreType.DMA(())],
        input_output_aliases={2: 0},
        compiler_params=_params("arbitrary"),
        name="moe_dispatch",
    )(dest.reshape(n_tok // tm, 1, 2 * tm), h2.reshape(n_tok // SUBLANES, SUBLANES, d),
      jnp.zeros((n_slots, d), h2.dtype))


def _ffn_kernel(te_ref, used_ref, x_ref, wg_ref, wu_ref, wd_ref, o_ref, wg_scr, wu_scr, wd_scr):
    i = pl.program_id(0)
    used = i < used_ref[0]
    new_expert = jnp.logical_or(i == 0, te_ref[i] != te_ref[jnp.maximum(i - 1, 0)])

    @pl.when(jnp.logical_and(used, new_expert))
    def _():
        wg_scr[...] = wg_ref[0].astype(BF16)
        wu_scr[...] = wu_ref[0].astype(BF16)
        wd_scr[...] = wd_ref[0].astype(BF16)

    @pl.when(used)
    def _():
        hi, lo = _unpack_halves(x_ref[...])
        x = jnp.concatenate([hi.astype(BF16), lo.astype(BF16)], axis=1)
        g = jnp.dot(x, wg_scr[...], preferred_element_type=F32)
        u = jnp.dot(x, wu_scr[...], preferred_element_type=F32)
        a = (g * jax.nn.sigmoid(g) * u).astype(BF16)
        o_ref[...] = _pack_halves(jnp.dot(a, wd_scr[...], preferred_element_type=F32))

    @pl.when(jnp.logical_not(used))
    def _():
        o_ref[...] = jnp.zeros_like(o_ref)


def _expert_ffn(tile_expert, n_used, xs, w_gate, w_up, w_down, n_tiles, layer):
    n_slots, half = xs.shape
    d = 2 * half
    te = EXPERT_TILE
    de = w_gate.shape[-1]
    grid_spec = pltpu.PrefetchScalarGridSpec(
        num_scalar_prefetch=2,
        grid=(n_tiles,),
        in_specs=[pl.BlockSpec((te, half), lambda i, te_ref, used: (i, 0)),
                  pl.BlockSpec((None, 1, d, de), lambda i, te_ref, used: (layer, te_ref[i], 0, 0)),
                  pl.BlockSpec((None, 1, d, de), lambda i, te_ref, used: (layer, te_ref[i], 0, 0)),
                  pl.BlockSpec((None, 1, de, d), lambda i, te_ref, used: (layer, te_ref[i], 0, 0))],
        out_specs=pl.BlockSpec((te, half), lambda i, te_ref, used: (i, 0)),
        scratch_shapes=[pltpu.VMEM((d, de), BF16), pltpu.VMEM((d, de), BF16), pltpu.VMEM((de, d), BF16)],
    )
    return pl.pallas_call(
        _ffn_kernel,
        grid_spec=grid_spec,
        out_shape=jax.ShapeDtypeStruct((n_slots, half), jnp.int32),
        compiler_params=_params("arbitrary"),
        name="moe_ffn",
    )(tile_expert, n_used, xs, w_gate, w_up, w_down)


def _combine_kernel(dest_ref, next_ref, wts_ref, x1_ref, mod_ref, gf_ref, ys_ref, o_ref, buf, sems, *, final):
    tm = x1_ref.shape[1]
    step = pl.program_id(0) * pl.num_programs(1) + pl.program_id(1)
    n_steps = pl.num_programs(0) * pl.num_programs(1)
    slot = step % 2

    def row_copy(src_row, into, k, group, sub):
        return pltpu.make_async_copy(_hbm_row(ys_ref, src_row), _vmem_row(buf.at[into, k], group, sub),
                                     sems.at[into])

    def gather(idx_ref, into):
        def start(g, carry):
            for u in range(SUBLANES):
                for k in range(2):
                    row_copy(idx_ref[0, 0, 2 * (g * SUBLANES + u) + k], into, k, g, u).start(priority=k)
            return carry

        lax.fori_loop(0, tm // SUBLANES, start, 0)

    @pl.when(step == 0)
    def _():
        gather(dest_ref, 0)

    @pl.when(step + 1 < n_steps)
    def _():
        gather(next_ref, 1 - slot)

    _wait_rows(lambda: row_copy(0, slot, 0, 0, 0), 2 * tm)
    wts = wts_ref[0]
    parts = []
    for k in range(2):
        hi, lo = _unpack_halves(buf[slot, k].reshape(tm, buf.shape[-1]))
        parts.append(wts[:, k:k + 1] * jnp.concatenate([hi, lo], axis=1))
    x2 = x1_ref[0] + mod_ref[0, 0, 5:6, :] * (parts[0] + parts[1])
    if final:
        ms = jnp.mean(x2 * x2, axis=-1, keepdims=True)
        x2 = x2 * lax.rsqrt(ms + RMS_EPS) * gf_ref[...]
    o_ref[0] = x2


def _combine(dest, wts, x1, modv, g_final, ys, n_lat, final):
    bsz, n_all, d = x1.shape
    tm = TOKEN_TILE
    lat_tiles = n_lat // tm
    all_tiles = n_all // tm
    tiles = lat_tiles if final else all_tiles
    n_out = n_lat if final else n_all

    def dest_row(step):
        step = jnp.minimum(step, bsz * tiles - 1)
        return (step // tiles) * all_tiles + step % tiles

    idx_spec = lambda ahead: pl.BlockSpec((1, 1, 2 * tm), lambda b, t: (dest_row(b * tiles + t + ahead), 0, 0),
                                          memory_space=pltpu.SMEM)
    dest3 = dest.reshape(bsz * all_tiles, 1, 2 * tm)
    return pl.pallas_call(
        functools.partial(_combine_kernel, final=final),
        grid=(bsz, tiles),
        in_specs=[idx_spec(0), idx_spec(1),
                  pl.BlockSpec((1, tm, LANES), lambda b, t: (b, t, 0)),
                  pl.BlockSpec((1, tm, d), lambda b, t: (b, t, 0)),
                  pl.BlockSpec((1, 1, 6, d), lambda b, t: (b, jnp.minimum(t // lat_tiles, 1), 0, 0)),
                  pl.BlockSpec((1, d), lambda b, t: (0, 0)),
                  pl.BlockSpec(memory_space=pl.ANY)],
        out_specs=pl.BlockSpec((1, tm, d), lambda b, t: (b, t, 0)),
        out_shape=jax.ShapeDtypeStruct((bsz, n_out, d), F32),
        scratch_shapes=[pltpu.VMEM((2, 2, tm // SUBLANES, SUBLANES, d // 2), jnp.int32),
                        pltpu.SemaphoreType.DMA((2,))],
        compiler_params=_params("arbitrary", "arbitrary"),
        name="moe_combine",
    )(dest3, dest3, wts, x1, modv, g_final, ys)


def kernel(x, c, ctx, c_ctx, w_mod, b_mod, g_mix, g_ffn, w_in, pool_w, pool_scale, diff_lambda, diff_norm_g, nat_rpb, gqa_q_norm, gqa_k_norm, w_branch, w_out, w_router_group, b_router_group, w_router_expert, b_router_expert, w_exp_gate, w_exp_up, w_exp_down, g_final):
    bsz, n_lat, d = x.shape
    n_ctx = ctx.shape[1]
    n_all = n_lat + n_ctx
    depth = w_mod.shape[0]
    tm = TOKEN_TILE
    assert n_lat % tm == 0 and n_ctx % tm == 0 and n_lat % GRID_W == 0 and n_lat // GRID_W >= NAT_WIN_R
    assert d == w_in.shape[1] and w_in.shape[2] == PROJ_COLS + N_BRANCH * d

    tokens = (x, ctx, 0)
    mod_rows = 16
    assert bsz + 1 <= mod_rows
    cc = jnp.zeros((mod_rows, d), F32).at[:bsz].set(c).at[bsz].set(c_ctx)
    mod = _modulation(cc, w_mod, b_mod)
    mod_lat = mod[:, :bsz].reshape(depth, bsz, 1, 6, d)
    mod_ctx = jnp.broadcast_to(mod[:, bsz].reshape(depth, 1, 1, 6, d), (depth, bsz, 1, 6, d))
    modv = jnp.concatenate([mod_lat, mod_ctx], axis=2)

    tabs = _rope_tables(n_lat, n_ctx, 256, DIFF_DQ) + _rope_tables(n_lat, n_ctx, 256, HEAD_DIM)
    head_of = np.arange(LANES) // HEAD_DIM
    ones_bd = jnp.asarray((head_of[:, None] == head_of[None, :]).astype(np.float32) / HEAD_DIM, BF16)
    group_of = np.arange(BRANCH_W) // POOL_GW
    pool_mask = jnp.asarray(group_of[:, None] == group_of[None, :])

    out = None
    for l in range(depth):
        final = l == depth - 1
        n_out = n_lat if final else n_all
        w_proj = w_in[l, :, :PROJ_COLS].astype(BF16)
        w_gate = w_in[l, :, PROJ_COLS:].astype(BF16)
        tile2 = lambda v: jnp.tile(v.reshape(1, HEAD_DIM), (1, LANES // HEAD_DIM))
        u, proj = _inproj(tokens, n_all, modv[l], g_mix[l].reshape(1, d), w_proj, tabs, ones_bd,
                          tile2(gqa_q_norm[l]), tile2(gqa_k_norm[l]), n_lat)
        pool_bd = jnp.where(pool_mask, jnp.tile(pool_w[l].reshape(BRANCH_W, POOL_GW), (1, BRANCH_W // POOL_GW)),
                            0.0).astype(BF16)
        b_diff = _diff_attention(proj, diff_lambda[l], tile2(diff_norm_g[l]), ones_bd, n_lat, l, n_out)
        b_nat = _nat_attention(proj, _nat_bias(nat_rpb[l]), n_lat, n_out)
        b_gqa = _gqa_attention(proj, n_lat, n_out)
        n_route = N_GROUPS + N_EXPERTS
        w_router = jnp.zeros((d, LANES), F32).at[:, :N_GROUPS].set(w_router_group[l])
        w_router = w_router.at[:, N_GROUPS:n_route].set(w_router_expert[l])
        b_router = jnp.zeros((1, LANES), F32).at[0, :N_GROUPS].set(b_router_group[l])
        b_router = b_router.at[0, N_GROUPS:n_route].set(b_router_expert[l])
        x1, h2, ids, wts = _merge(tokens, n_out, modv[l], g_mix[l].reshape(1, d), g_ffn[l].reshape(1, d),
                                  u, pool_bd, pool_scale[l].reshape(1, BRANCH_W),
                                  (b_diff, b_nat, b_gqa), w_gate, w_branch[l].astype(BF16),
                                  w_out[l].astype(BF16), w_router, b_router, n_lat)
        dest, tile_expert, n_used, n_tiles = _slot_plan(ids[:, :, :2])
        xs = _dispatch(dest, h2.reshape(bsz * n_out, d // 2), n_tiles * EXPERT_TILE)
        ys = _expert_ffn(tile_expert, n_used, xs, w_exp_gate, w_exp_up, w_exp_down, n_tiles, l)
        out = _combine(dest, wts, x1, modv[l], g_final.reshape(1, d), ys, n_lat, final)
        tokens = (out, out, n_lat // tm)
    return out
```
